```python
import math
import jax
import jax.numpy as jnp
from jax import lax
import numpy as np

D_MODEL = 2048
BATCH = 1
SEQ = 8192
DEPTH = 4

GRID_W = 64
CTX_LEN = 256
EPS = 1e-6

DA_HEADS = 8
DA_QK_DIM = 64
DA_V_DIM = 2 * DA_QK_DIM
DA_WIDTH = DA_HEADS * DA_V_DIM
Q_BLOCK = 128
ROPE_THETA = 10000.0
NA_HEADS = 8
NA_HEAD_DIM = 128
NA_WIDTH = NA_HEADS * NA_HEAD_DIM
NA_WIN_ROWS = 8
NA_WIN_COLS = 16
ATT_IN = 3 * DA_WIDTH + 3 * NA_WIDTH
HG_HEADS = 16
HG_KEY_DIM = D_MODEL // HG_HEADS
HG_VAL_DIM = D_MODEL // HG_HEADS
HG_CHUNK = 64
REC_IN = 5 * D_MODEL
N_GROUPS = 4
EXPERTS_PER_GROUP = 8
N_EXPERTS = N_GROUPS * EXPERTS_PER_GROUP
TOP_K = 2
D_EXPERT = 768
MOE_BLOCK = 128

N_ATT_LAYERS = (DEPTH + 1) // 2
N_REC_LAYERS = DEPTH // 2

kernel_name = 'hybrid_diffattn_natten_hgrn2_hmoe_dit'


def rms_norm(x, g):
    xf = x.astype(jnp.float32)
    y = xf * lax.rsqrt(jnp.mean(xf * xf, axis=-1, keepdims=True) + EPS)
    return (y * g.astype(jnp.float32)).astype(x.dtype)


def modulate(h, shift, scale):
    return h * (1 + scale) + shift


def _axial_rope_tables(n_tok):
    t = jnp.arange(n_tok, dtype=jnp.int32)
    rows = (t // GRID_W).astype(jnp.float32)
    cols = (t % GRID_W).astype(jnp.float32)
    n_freq = DA_QK_DIM // 4
    inv_freq = ROPE_THETA ** (-jnp.arange(n_freq, dtype=jnp.float32) / n_freq)
    ang = jnp.stack([rows[:, None] * inv_freq, cols[:, None] * inv_freq], axis=1)
    return jnp.cos(ang), jnp.sin(ang)


def _apply_axial_rope(x, cos, sin):
    shp = x.shape
    xr = x.reshape(shp[:-1] + (2, 2, DA_QK_DIM // 4))
    x1, x2 = xr[..., 0, :], xr[..., 1, :]
    cos = cos.astype(x.dtype)
    sin = sin.astype(x.dtype)
    out = jnp.stack([x1 * cos - x2 * sin, x2 * cos + x1 * sin], axis=-2)
    return out.reshape(shp)


def _diff_lambda(lam_p, layer_idx):
    lam_init = 0.8 - 0.6 * math.exp(-0.3 * layer_idx)
    lf = lam_p.astype(jnp.float32)
    lam = jnp.exp(jnp.sum(lf[0] * lf[1])) - jnp.exp(jnp.sum(lf[2] * lf[3])) + lam_init
    return lam, lam_init


def _diff_attend(q, k, v, lam):
    s = jnp.einsum('bhmqd,bhmkd->bhmqk', q, k).astype(jnp.float32)
    p = jax.nn.softmax(s, axis=-1)
    a = p[:, :, 0] - lam * p[:, :, 1]
    return jnp.einsum('bhqk,bhkv->bhqv', a.astype(v.dtype), v)


def _plain_attend(q, k, v):
    s = jnp.einsum('bhqd,bhkd->bhqk', q, k).astype(jnp.float32)
    p = jax.nn.softmax(s, axis=-1).astype(v.dtype)
    return jnp.einsum('bhqk,bhkd->bhqd', p, v)


def _neighbourhood_attend(q_lat, k_lat, v_lat, k_ctx, v_ctx, rpb):
    b_, h_, n_, d_ = q_lat.shape
    rows = n_ // GRID_W
    wr = min(NA_WIN_ROWS, rows)
    wc = NA_WIN_COLS
    c_ = k_ctx.shape[2]
    qg = q_lat.reshape(b_, h_, rows, GRID_W, d_)
    kg = k_lat.reshape(b_, h_, rows, GRID_W, d_)
    vg = v_lat.reshape(b_, h_, rows, GRID_W, d_)
    cols = np.arange(GRID_W)
    col_idx = np.clip(cols - wc // 2, 0, GRID_W - wc)[:, None] + np.arange(wc)[None, :]
    col_off = col_idx - cols[:, None]

    def row_block(r):
        rs = jnp.clip(r - wr // 2, 0, rows - wr)
        row_off = rs + jnp.arange(wr) - r
        bias = rpb[:, (row_off + NA_WIN_ROWS - 1)[None, :, None], (col_off + wc - 1)[:, None, :]]
        kr = lax.dynamic_slice_in_dim(kg, rs, wr, axis=2)[:, :, :, col_idx]
        vr = lax.dynamic_slice_in_dim(vg, rs, wr, axis=2)[:, :, :, col_idx]
        qr = lax.dynamic_index_in_dim(qg, r, axis=2, keepdims=False)
        s_loc = jnp.einsum('bhwd,bhiwjd->bhwij', qr, kr).astype(jnp.float32) + bias.astype(jnp.float32)
        s_ctx = jnp.einsum('bhwd,bhcd->bhwc', qr, k_ctx).astype(jnp.float32)
        s = jnp.concatenate([s_ctx, s_loc.reshape(b_, h_, GRID_W, wr * wc)], axis=-1)
        p = jax.nn.softmax(s, axis=-1).astype(v_ctx.dtype)
        p_ctx = p[..., :c_]
        p_loc = p[..., c_:].reshape(b_, h_, GRID_W, wr, wc)
        return jnp.einsum('bhwc,bhcd->bhwd', p_ctx, v_ctx) + jnp.einsum('bhwij,bhiwjd->bhwd', p_loc, vr)

    out = lax.map(row_block, jnp.arange(rows))
    return out.transpose(1, 2, 0, 3, 4).reshape(b_, h_, n_, d_)


def _split_att(p):
    b_, n_, _ = p.shape
    o0, o1, o2, o3 = 0, DA_WIDTH, 2 * DA_WIDTH, 3 * DA_WIDTH
    o4, o5, o6 = o3 + NA_WIDTH, o3 + 2 * NA_WIDTH, ATT_IN
    qa = p[..., o0:o1].reshape(b_, n_, DA_HEADS, 2, DA_QK_DIM).transpose(0, 2, 3, 1, 4)
    ka = p[..., o1:o2].reshape(b_, n_, DA_HEADS, 2, DA_QK_DIM).transpose(0, 2, 3, 1, 4)
    va = p[..., o2:o3].reshape(b_, n_, DA_HEADS, DA_V_DIM).transpose(0, 2, 1, 3)
    qn = p[..., o3:o4].reshape(b_, n_, NA_HEADS, NA_HEAD_DIM).transpose(0, 2, 1, 3)
    kn = p[..., o4:o5].reshape(b_, n_, NA_HEADS, NA_HEAD_DIM).transpose(0, 2, 1, 3)
    vn = p[..., o5:o6].reshape(b_, n_, NA_HEADS, NA_HEAD_DIM).transpose(0, 2, 1, 3)
    return qa, ka, va, qn, kn, vn


def attn_mixer(h_lat, h_ctx, w_in, w_out, lam_p, subln_g, rpb, layer_idx, need_ctx_out):
    b_, n_, _ = h_lat.shape
    qa_l, ka_l, va_l, qn_l, kn_l, vn_l = _split_att(h_lat @ w_in)
    qa_c, ka_c, va_c, qn_c, kn_c, vn_c = _split_att(h_ctx @ w_in)
    lam, lam_init = _diff_lambda(lam_p, layer_idx)
    sa = DA_QK_DIM ** -0.5
    sn = NA_HEAD_DIM ** -0.5
    cos, sin = _axial_rope_tables(n_)
    qa_l = _apply_axial_rope(qa_l, cos, sin) * sa
    ka_l = _apply_axial_rope(ka_l, cos, sin)
    ka_all = jnp.concatenate([ka_c, ka_l], axis=3)
    va_all = jnp.concatenate([va_c, va_l], axis=2)
    nqb = n_ // Q_BLOCK
    qb = qa_l.reshape(b_, DA_HEADS, 2, nqb, Q_BLOCK, DA_QK_DIM).transpose(3, 0, 1, 2, 4, 5)
    da_l = lax.map(lambda qq: _diff_attend(qq, ka_all, va_all, lam), qb)
    da_l = da_l.transpose(1, 2, 0, 3, 4).reshape(b_, DA_HEADS, n_, DA_V_DIM)
    na_l = _neighbourhood_attend(qn_l * sn, kn_l, vn_l, kn_c, vn_c, rpb)

    def merge(da, na):
        da = rms_norm(da, subln_g) * (1.0 - lam_init)
        n = da.shape[2]
        cat = jnp.concatenate([da.transpose(0, 2, 1, 3).reshape(b_, n, DA_WIDTH),
                               na.transpose(0, 2, 1, 3).reshape(b_, n, NA_WIDTH)], axis=-1)
        return cat @ w_out

    y_lat = merge(da_l, na_l)
    y_ctx = None
    if need_ctx_out:
        da_c = _diff_attend(qa_c * sa, ka_c, va_c, lam)
        na_c = _plain_attend(qn_c * sn, kn_c, vn_c)
        y_ctx = merge(da_c, na_c)
    return y_lat, y_ctx


def _gla_chunk_scan(q, k, v, logf, s0):
    b_, h_, l_, _ = q.shape
    nc = l_ // HG_CHUNK

    def chunks(a):
        return a.reshape(b_, h_, nc, HG_CHUNK, a.shape[-1]).transpose(2, 0, 1, 3, 4)

    lower = jnp.tril(jnp.ones((HG_CHUNK, HG_CHUNK), dtype=bool))[:, :, None]

    def step(state, xs):
        qc, kc, vc, gc = xs
        cum = jnp.cumsum(gc, axis=2)
        o_inter = jnp.einsum('bhtk,bhkv->bhtv', qc * jnp.exp(cum), state)
        rel = cum[:, :, :, None, :] - cum[:, :, None, :, :]
        decay = jnp.exp(jnp.where(lower, rel, -jnp.inf))
        att = jnp.einsum('bhtk,bhsk,bhtsk->bhts', qc, kc, decay)
        o = o_inter + jnp.einsum('bhts,bhsv->bhtv', att, vc)
        last = cum[:, :, -1:, :]
        state = jnp.exp(last[:, :, 0, :])[..., None] * state + jnp.einsum('bhsk,bhsv->bhkv', kc * jnp.exp(last - cum), vc)
        return state, o

    s_fin, o = lax.scan(step, s0, (chunks(q), chunks(k), chunks(v), chunks(logf)))
    o = o.transpose(1, 2, 0, 3, 4).reshape(b_, h_, l_, v.shape[-1])
    return o, s_fin


def _split_rec(p):
    b_, n_, _ = p.shape
    q, i, ff, fb, g = jnp.split(p, 5, axis=-1)
    heads = lambda a: a.reshape(b_, n_, HG_HEADS, -1).transpose(0, 2, 1, 3)
    return heads(q), heads(i), heads(ff), heads(fb), heads(g)


def rec_mixer(h_lat, h_ctx, w_in, w_out, lb, gnorm_g, need_ctx_out):
    f32 = jnp.float32
    dt = h_lat.dtype
    b_ = h_lat.shape[0]
    q_l, i_l, ff_l, fb_l, g_l = _split_rec(h_lat @ w_in)
    q_c, i_c, ff_c, fb_c, g_c = _split_rec(h_ctx @ w_in)

    def gates(z, lb_d):
        lbh = lb_d.reshape(HG_HEADS, 1, HG_KEY_DIM)
        f = lbh + (1.0 - lbh) * jax.nn.sigmoid(z.astype(f32))
        return 1.0 - f, jnp.log(f)

    qs = lambda a: jax.nn.silu(a.astype(f32)) * HG_KEY_DIM ** -0.5
    Qc, Ql = qs(q_c), qs(q_l)
    Vc, Vl = i_c.astype(f32), i_l.astype(f32)
    s0 = jnp.zeros((b_, HG_HEADS, HG_KEY_DIM, HG_VAL_DIM), f32)
    flip = lambda a: jnp.flip(a, axis=2)
    kc, gc = gates(ff_c, lb[0])
    kl, gl = gates(ff_l, lb[0])
    oc_f, sc_f = _gla_chunk_scan(Qc, kc, Vc, gc, s0)
    ol_f, _ = _gla_chunk_scan(Ql, kl, Vl, gl, sc_f)
    kc, gc = gates(fb_c, lb[1])
    kl, gl = gates(fb_l, lb[1])
    oc_b, sc_b = _gla_chunk_scan(flip(Qc), flip(kc), flip(Vc), flip(gc), s0)
    ol_b, _ = _gla_chunk_scan(flip(Ql), flip(kl), flip(Vl), flip(gl), sc_b)

    def readout(o, g):
        o = rms_norm(o, gnorm_g) * jax.nn.silu(g.astype(f32))
        n = o.shape[2]
        return o.transpose(0, 2, 1, 3).reshape(b_, n, D_MODEL).astype(dt) @ w_out

    y_lat = readout(ol_f + flip(ol_b), g_l)
    y_ctx = None
    if need_ctx_out:
        y_ctx = readout(oc_f + flip(oc_b), g_c)
    return y_lat, y_ctx


def hier_moe(h, w_grp, b_grp, w_rt, b_rt, w_gate, w_up, w_down):
    t_ = h.shape[0]
    g_logits = (h @ w_grp).astype(jnp.float32) + b_grp.astype(jnp.float32)
    g_prob = jax.nn.softmax(g_logits, axis=-1)
    g_sel = jnp.argmax(g_logits, axis=-1).astype(jnp.int32)
    g_w = jnp.take_along_axis(g_prob, g_sel[:, None], axis=1)[:, 0]
    e_logits = (h @ w_rt).astype(jnp.float32).reshape(t_, N_GROUPS, EXPERTS_PER_GROUP)
    e_logits = e_logits + b_rt.astype(jnp.float32).reshape(N_GROUPS, EXPERTS_PER_GROUP)
    e_logits = jnp.take_along_axis(e_logits, g_sel[:, None, None], axis=1)[:, 0]
    top_v, top_i = lax.top_k(e_logits, TOP_K)
    weights = g_w[:, None] * jax.nn.softmax(top_v, axis=-1)
    expert = g_sel[:, None] * EXPERTS_PER_GROUP + top_i.astype(jnp.int32)
    n_as = t_ * TOP_K
    eid = expert.reshape(-1)
    gate = weights.reshape(-1)
    tok = jnp.repeat(jnp.arange(t_, dtype=jnp.int32), TOP_K)
    order = jnp.argsort(eid).astype(jnp.int32)
    eid_s = eid[order]
    counts = jnp.zeros((N_EXPERTS,), jnp.int32).at[eid].add(1)
    padded = (counts + MOE_BLOCK - 1) // MOE_BLOCK * MOE_BLOCK
    start = jnp.cumsum(counts) - counts
    pstart = jnp.cumsum(padded) - padded
    dest = pstart[eid_s] + jnp.arange(n_as, dtype=jnp.int32) - start[eid_s]
    n_blk = -(-n_as // MOE_BLOCK) + N_EXPERTS
    n_slot = n_blk * MOE_BLOCK
    slot_src = jnp.full((n_slot,), n_as, jnp.int32).at[dest].set(order)
    slot_tok = jnp.concatenate([tok, jnp.zeros((1,), jnp.int32)])[slot_src]
    slot_gate = jnp.concatenate([gate, jnp.zeros((1,), gate.dtype)])[slot_src]
    blk_expert = jnp.searchsorted(jnp.cumsum(padded), jnp.arange(n_blk, dtype=jnp.int32) * MOE_BLOCK, side='right')
    blk_expert = jnp.minimum(blk_expert, N_EXPERTS - 1).astype(jnp.int32)
    xs = h[slot_tok].reshape(n_blk, MOE_BLOCK, h.shape[-1])

    def expert_block(args):
        xb, e = args
        hid = jax.nn.silu(xb @ w_gate[e]) * (xb @ w_up[e])
        return hid @ w_down[e]

    ys = lax.map(expert_block, (xs, blk_expert)).reshape(n_slot, h.shape[-1])
    return jnp.zeros_like(h).at[slot_tok].add(ys * slot_gate[:, None].astype(ys.dtype))


def setup_inputs(seed: int = 0) -> dict:
    key = jax.random.key(seed)
    ks = jax.random.split(key, 25)
    f32 = jnp.float32
    D = D_MODEL
    nrm = lambda k, shape, s: jax.random.normal(k, shape, f32) * s
    return {
        'x': nrm(ks[0], (BATCH, SEQ, D), 1.0),
        'c': nrm(ks[1], (BATCH, D), 1.0),
        'ctx': nrm(ks[2], (BATCH, CTX_LEN, D), 1.0),
        'c_ctx': nrm(ks[3], (D,), 1.0),
        'w_mod': nrm(ks[4], (DEPTH, D, 6 * D), 0.5 * D ** -0.5),
        'b_mod': nrm(ks[5], (DEPTH, 6 * D), 0.02),
        'norm1_g': 1.0 + nrm(ks[6], (DEPTH, D), 0.02),
        'norm2_g': 1.0 + nrm(ks[7], (DEPTH, D), 0.02),
        'att_w_in': nrm(ks[8], (N_ATT_LAYERS, D, ATT_IN), D ** -0.5),
        'att_w_out': nrm(ks[9], (N_ATT_LAYERS, DA_WIDTH + NA_WIDTH, D), (DA_WIDTH + NA_WIDTH) ** -0.5),
        'att_lambda': nrm(ks[10], (N_ATT_LAYERS, 4, DA_QK_DIM), 0.1),
        'att_subln_g': 1.0 + nrm(ks[11], (N_ATT_LAYERS, DA_V_DIM), 0.02),
        'att_rpb': nrm(ks[12], (N_ATT_LAYERS, NA_HEADS, 2 * NA_WIN_ROWS - 1, 2 * NA_WIN_COLS - 1), 0.1),
        'rec_w_in': nrm(ks[13], (N_REC_LAYERS, D, REC_IN), D ** -0.5),
        'rec_w_out': nrm(ks[14], (N_REC_LAYERS, D, D), D ** -0.5),
        'rec_lb_logits': nrm(ks[15], (DEPTH, 2, D), 0.5),
        'rec_gnorm_g': 1.0 + nrm(ks[16], (N_REC_LAYERS, HG_VAL_DIM), 0.02),
        'moe_w_group': nrm(ks[17], (DEPTH, D, N_GROUPS), D ** -0.5),
        'moe_b_group': nrm(ks[18], (DEPTH, N_GROUPS), 0.01),
        'moe_w_router': nrm(ks[19], (DEPTH, D, N_EXPERTS), D ** -0.5),
        'moe_b_router': nrm(ks[20], (DEPTH, N_EXPERTS), 0.01),
        'moe_w_gate': nrm(ks[21], (DEPTH, N_EXPERTS, D, D_EXPERT), D ** -0.5),
        'moe_w_up': nrm(ks[22], (DEPTH, N_EXPERTS, D, D_EXPERT), D ** -0.5),
        'moe_w_down': nrm(ks[23], (DEPTH, N_EXPERTS, D_EXPERT, D), D_EXPERT ** -0.5),
        'final_norm_g': 1.0 + nrm(ks[24], (D,), 0.02),
    }


def reference(x, c, ctx, c_ctx, w_mod, b_mod, norm1_g, norm2_g, att_w_in, att_w_out, att_lambda,
              att_subln_g, att_rpb, rec_w_in, rec_w_out, rec_lb_logits, rec_gnorm_g, moe_w_group,
              moe_b_group, moe_w_router, moe_b_router, moe_w_gate, moe_w_up, moe_w_down, final_norm_g):
    b_, n_, d_ = x.shape
    c_len = ctx.shape[1]
    lbp = jax.nn.softmax(rec_lb_logits.astype(jnp.float32), axis=0)
    lbs = jnp.cumsum(lbp, axis=0) - lbp[0:1]
    h_lat, h_ctx = x, ctx
    for l in range(DEPTH):
        need_ctx = l < DEPTH - 1
        j = l // 2
        mod_l = jnp.split((jax.nn.silu(c) @ w_mod[l] + b_mod[l])[:, None, :], 6, axis=-1)
        mod_c = jnp.split(jax.nn.silu(c_ctx) @ w_mod[l] + b_mod[l], 6, axis=-1)
        a_lat = modulate(rms_norm(h_lat, norm1_g[l]), mod_l[0], mod_l[1])
        a_ctx = modulate(rms_norm(h_ctx, norm1_g[l]), mod_c[0], mod_c[1])
        if l % 2 == 0:
            y_lat, y_ctx = attn_mixer(a_lat, a_ctx, att_w_in[j], att_w_out[j], att_lambda[j],
                                      att_subln_g[j], att_rpb[j], l, need_ctx)
        else:
            y_lat, y_ctx = rec_mixer(a_lat, a_ctx, rec_w_in[j], rec_w_out[j], lbs[l], rec_gnorm_g[j], need_ctx)
        h_lat = h_lat + mod_l[2] * y_lat
        m_lat = modulate(rms_norm(h_lat, norm2_g[l]), mod_l[3], mod_l[4])
        moe_p = (moe_w_group[l], moe_b_group[l], moe_w_router[l], moe_b_router[l],
                 moe_w_gate[l], moe_w_up[l], moe_w_down[l])
        if need_ctx:
            h_ctx = h_ctx + mod_c[2] * y_ctx
            m_ctx = modulate(rms_norm(h_ctx, norm2_g[l]), mod_c[3], mod_c[4])
            tokens = jnp.concatenate([m_ctx.reshape(b_ * c_len, d_), m_lat.reshape(b_ * n_, d_)], axis=0)
            out = hier_moe(tokens, *moe_p)
            h_ctx = h_ctx + mod_c[5] * out[:b_ * c_len].reshape(b_, c_len, d_)
            h_lat = h_lat + mod_l[5] * out[b_ * c_len:].reshape(b_, n_, d_)
        else:
            h_lat = h_lat + mod_l[5] * hier_moe(m_lat.reshape(b_ * n_, d_), *moe_p).reshape(b_, n_, d_)
    return rms_norm(h_lat, final_norm_g)
```

```python
import functools
import math

import numpy as np
import jax
import jax.numpy as jnp
from jax import lax
from jax.experimental import pallas as pl
from jax.experimental.pallas import tpu as pltpu

F32 = jnp.float32
BF16 = jnp.bfloat16

GRID_W = 64
EPS = 1e-6
DA_HEADS = 8
DA_QK_DIM = 64
NA_HEADS = 8
NA_WIN_ROWS = 8
NA_WIN_COLS = 16
ROPE_THETA = 10000.0
HG_HEADS = 16
N_GROUPS = 4
EXPERTS_PER_GROUP = 8
TOP_K = 2

LANES = 128
ROW_TILE = 256
HG_CHUNK = 64
HG_SUB = 16
NA_QROWS = 4
NA_KROWS = 12
MOE_BM = 256
NEG = -1e30
VMEM_LIMIT = 56 << 20


def _cp(*sem):
    return pltpu.CompilerParams(dimension_semantics=sem, vmem_limit_bytes=VMEM_LIMIT)


def _pick(n, cands):
    for c in cands:
        if n % c == 0:
            return c
    raise ValueError(f"no tile for {n} in {cands}")


def _silu(x):
    return x * jax.nn.sigmoid(x)


def _rms(x, g):
    return x * lax.rsqrt(jnp.mean(x * x, axis=-1, keepdims=True) + EPS) * g


def _mod_kernel(c_ref, w_ref, b_ref, o_ref):
    s = _silu(c_ref[...]).astype(BF16)
    o_ref[0] = jnp.dot(s, w_ref[0].astype(BF16), preferred_element_type=F32) + b_ref[0]


def _modulation(cvec, w_mod, b_mod):
    depth, d, n = w_mod.shape
    tn = _pick(n, (1024, 512, 256, 128))
    return pl.pallas_call(
        _mod_kernel,
        grid=(depth, n // tn),
        in_specs=[pl.BlockSpec((8, d), lambda l, j: (0, 0)),
                  pl.BlockSpec((1, d, tn), lambda l, j: (l, 0, j)),
                  pl.BlockSpec((1, 1, tn), lambda l, j: (l, 0, j))],
        out_specs=pl.BlockSpec((1, 8, tn), lambda l, j: (l, 0, j)),
        out_shape=jax.ShapeDtypeStruct((depth, 8, n), F32),
        compiler_params=_cp("arbitrary", "arbitrary"),
        name="modulation",
    )(cvec, w_mod, b_mod.reshape(depth, 1, n))


def _mod_row(ref, n_lat_tiles):
    r = (pl.program_id(0) >= n_lat_tiles).astype(jnp.int32)
    return ref[pl.ds(r, 1), :]


def _ln_mod_kernel(h_ref, g_ref, sh_ref, sc_ref, a_ref, *, n_lat_tiles):
    y = _rms(h_ref[...], g_ref[...])
    a = y * (1.0 + _mod_row(sc_ref, n_lat_tiles)) + _mod_row(sh_ref, n_lat_tiles)
    a_ref[...] = a.astype(a_ref.dtype)


def _res_ln_mod_kernel(h_ref, y_ref, gt_ref, g_ref, sh_ref, sc_ref, ho_ref, a_ref, *, n_lat_tiles):
    h = h_ref[...] + _mod_row(gt_ref, n_lat_tiles) * y_ref[...]
    ho_ref[...] = h
    a = _rms(h, g_ref[...]) * (1.0 + _mod_row(sc_ref, n_lat_tiles)) + _mod_row(sh_ref, n_lat_tiles)
    a_ref[...] = a.astype(a_ref.dtype)


def _res_ln_mod_route_kernel(h_ref, y_ref, gt_ref, g_ref, sh_ref, sc_ref, wr_ref, br_ref,
                             ho_ref, a_ref, lg_ref, *, n_lat_tiles):
    h = h_ref[...] + _mod_row(gt_ref, n_lat_tiles) * y_ref[...]
    ho_ref[...] = h
    a = _rms(h, g_ref[...]) * (1.0 + _mod_row(sc_ref, n_lat_tiles)) + _mod_row(sh_ref, n_lat_tiles)
    a_ref[...] = a.astype(a_ref.dtype)
    lg_ref[...] = jnp.dot(a, wr_ref[...], preferred_element_type=F32,
                          precision=lax.Precision.HIGHEST) + br_ref[...]


def _res_final_kernel(h_ref, y_ref, gt_ref, g_ref, o_ref):
    h = h_ref[...] + gt_ref[pl.ds(0, 1), :] * y_ref[...]
    o_ref[...] = _rms(h, g_ref[...])


def _row_spec(d):
    return pl.BlockSpec((ROW_TILE, d), lambda i: (i, 0))


def _vec_spec(d):
    return pl.BlockSpec((1, d), lambda i: (0, 0))


def _mod_spec(d, k):
    return pl.BlockSpec((8, d), lambda i: (0, k))


def _ln_mod(h, g, mod, k_shift, k_scale, n_lat):
    t, d = h.shape
    return pl.pallas_call(
        functools.partial(_ln_mod_kernel, n_lat_tiles=n_lat // ROW_TILE),
        grid=(t // ROW_TILE,),
        in_specs=[_row_spec(d), _vec_spec(d), _mod_spec(d, k_shift), _mod_spec(d, k_scale)],
        out_specs=_row_spec(d),
        out_shape=jax.ShapeDtypeStruct((t, d), BF16),
        compiler_params=_cp("arbitrary"),
        name="ln_mod",
    )(h, g.reshape(1, d), mod, mod)


def _res_ln_mod(h, y, g, mod_gate, k_gate, mod_next, k_shift, k_scale, n_lat):
    t, d = h.shape
    return pl.pallas_call(
        functools.partial(_res_ln_mod_kernel, n_lat_tiles=n_lat // ROW_TILE),
        grid=(t // ROW_TILE,),
        in_specs=[_row_spec(d), _row_spec(d), _mod_spec(d, k_gate), _vec_spec(d),
                  _mod_spec(d, k_shift), _mod_spec(d, k_scale)],
        out_specs=[_row_spec(d), _row_spec(d)],
        out_shape=[jax.ShapeDtypeStruct((t, d), F32), jax.ShapeDtypeStruct((t, d), BF16)],
        compiler_params=_cp("arbitrary"),
        name="res_ln_mod",
    )(h, y, mod_gate, g.reshape(1, d), mod_next, mod_next)


def _res_ln_mod_route(h, y, g, mod, k_gate, k_shift, k_scale, w_route, b_route, n_lat):
    t, d = h.shape
    return pl.pallas_call(
        functools.partial(_res_ln_mod_route_kernel, n_lat_tiles=n_lat // ROW_TILE),
        grid=(t // ROW_TILE,),
        in_specs=[_row_spec(d), _row_spec(d), _mod_spec(d, k_gate), _vec_spec(d),
                  _mod_spec(d, k_shift), _mod_spec(d, k_scale),
                  pl.BlockSpec((d, LANES), lambda i: (0, 0)), _vec_spec(LANES)],
        out_specs=[_row_spec(d), _row_spec(d), _row_spec(LANES)],
        out_shape=[jax.ShapeDtypeStruct((t, d), F32), jax.ShapeDtypeStruct((t, d), BF16),
                   jax.ShapeDtypeStruct((t, LANES), F32)],
        compiler_params=_cp("arbitrary"),
        name="res_ln_mod_route",
    )(h, y, mod, g.reshape(1, d), mod, mod, w_route, b_route)


def _res_final(h, y, g, mod, k_gate, n_lat):
    d = h.shape[1]
    return pl.pallas_call(
        _res_final_kernel,
        grid=(n_lat // ROW_TILE,),
        in_specs=[_row_spec(d), _row_spec(d), _mod_spec(d, k_gate), _vec_spec(d)],
        out_specs=_row_spec(d),
        out_shape=jax.ShapeDtypeStruct((n_lat, d), F32),
        compiler_params=_cp("arbitrary"),
        name="res_final",
    )(h, y, mod, g.reshape(1, d))


def _matmul_kernel(a_ref, w_ref, o_ref, wb_ref):
    @pl.when(pl.program_id(1) == 0)
    def _():
        wb_ref[...] = w_ref[...].astype(BF16)

    o_ref[...] = jnp.dot(a_ref[...], wb_ref[...], preferred_element_type=F32).astype(o_ref.dtype)


def _matmul(a, w, out_dtype):
    m, k = a.shape
    n = w.shape[1]
    tm = _pick(m, (768, 640, 512, 256))
    tn = _pick(n, (1024, 512, 256, 128))
    return pl.pallas_call(
        _matmul_kernel,
        grid=(n // tn, m // tm),
        in_specs=[pl.BlockSpec((tm, k), lambda j, i: (i, 0)),
                  pl.BlockSpec((k, tn), lambda j, i: (0, j))],
        out_specs=pl.BlockSpec((tm, tn), lambda j, i: (i, j)),
        out_shape=jax.ShapeDtypeStruct((m, n), out_dtype),
        scratch_shapes=[pltpu.VMEM((k, tn), BF16)],
        compiler_params=_cp("arbitrary", "arbitrary"),
        name="matmul",
    )(a, w)


def _attn_prep_kernel(p_ref, cos_ref, sin_ref, o_ref, *, daw, naw, sa, sn):
    cos = cos_ref[...]
    sin = sin_ref[...]
    lane = lax.broadcasted_iota(jnp.int32, cos.shape, 1)
    first_half = (lane % (DA_QK_DIM // 2)) < (DA_QK_DIM // 4)
    quarter = DA_QK_DIM // 4
    for j in range((3 * daw + 3 * naw) // LANES):
        c0 = j * LANES
        x = p_ref[:, c0:c0 + LANES]
        if c0 < 2 * daw:
            partner = jnp.where(first_half, pltpu.roll(x, LANES - quarter, 1), pltpu.roll(x, quarter, 1))
            x = x * cos + partner * sin
            if c0 < daw:
                x = x * sa
        elif 3 * daw <= c0 < 3 * daw + naw:
            x = x * sn
        o_ref[:, c0:c0 + LANES] = x.astype(o_ref.dtype)


def _rope_tables(n_lat, n_ctx):
    t = jnp.arange(n_lat, dtype=jnp.int32)
    rows = (t // GRID_W).astype(F32)
    cols = (t % GRID_W).astype(F32)
    n_freq = DA_QK_DIM // 4
    inv_freq = ROPE_THETA ** (-jnp.arange(n_freq, dtype=F32) / n_freq)
    ang_r = rows[:, None] * inv_freq
    ang_c = cols[:, None] * inv_freq
    ang = jnp.concatenate([ang_r, ang_r, ang_c, ang_c], axis=1)
    sign = jnp.concatenate([-jnp.ones((n_freq,), F32), jnp.ones((n_freq,), F32)] * 2)
    cos = jnp.cos(ang)
    sin = jnp.sin(ang) * sign
    reps = LANES // DA_QK_DIM
    cos = jnp.concatenate([jnp.tile(cos, (1, reps)), jnp.ones((n_ctx, LANES), F32)], axis=0)
    sin = jnp.concatenate([jnp.tile(sin, (1, reps)), jnp.zeros((n_ctx, LANES), F32)], axis=0)
    return cos, sin


def _attn_prep(p, cos, sin, daw, naw):
    t, n = p.shape
    kern = functools.partial(_attn_prep_kernel, daw=daw, naw=naw, sa=DA_QK_DIM ** -0.5, sn=LANES ** -0.5)
    return pl.pallas_call(
        kern,
        grid=(t // ROW_TILE,),
        in_specs=[_row_spec(n), _row_spec(LANES), _row_spec(LANES)],
        out_specs=_row_spec(n),
        out_shape=jax.ShapeDtypeStruct((t, n), BF16),
        compiler_params=_cp("arbitrary"),
        name="attn_prep",
    )(p, cos, sin)


def _diff_attn_kernel(lam_ref, q_ref, k_ref, v_ref, g_ref, o_ref, qm_ref, m_ref, l_ref, acc_ref, *, nk):
    kv = pl.program_id(2)

    @pl.when(kv == 0)
    def _():
        q = q_ref[...]
        lane = lax.broadcasted_iota(jnp.int32, q.shape, 1)
        zero = jnp.zeros_like(q)
        qm_ref[0] = jnp.where(lane < DA_QK_DIM, q, zero)
        qm_ref[1] = jnp.where(lane >= DA_QK_DIM, q, zero)
        m_ref[...] = jnp.full(m_ref.shape, NEG, F32)
        l_ref[...] = jnp.zeros(l_ref.shape, F32)
        acc_ref[...] = jnp.zeros(acc_ref.shape, F32)

    k = k_ref[...]
    v = v_ref[...]
    for mi in range(2):
        s = lax.dot_general(qm_ref[mi], k, (((1,), (1,)), ((), ())), preferred_element_type=F32)
        m_prev = m_ref[mi]
        m_new = jnp.maximum(m_prev, jnp.max(s, axis=1, keepdims=True))
        alpha = jnp.exp(m_prev - m_new)
        p = jnp.exp(s - m_new)
        l_ref[mi] = alpha * l_ref[mi] + jnp.sum(p, axis=1, keepdims=True)
        acc_ref[mi] = alpha * acc_ref[mi] + jnp.dot(p.astype(BF16), v, preferred_element_type=F32)
        m_ref[mi] = m_new

    @pl.when(kv == nk - 1)
    def _():
        o = acc_ref[0] / l_ref[0] - lam_ref[0] * (acc_ref[1] / l_ref[1])
        o_ref[...] = (_rms(o, g_ref[...]) * lam_ref[1]).astype(o_ref.dtype)


def _diff_attn(qkv, lam2, subln_g, q_row0, n_q, k_row0, n_k, heads):
    tq = _pick(n_q, (1024, 512, 256))
    tk = _pick(n_k, (1408, 1280, 1024, 768, 512, 256))
    assert q_row0 % tq == 0 and k_row0 % tk == 0
    nq, nk = n_q // tq, n_k // tk
    qb, kb = q_row0 // tq, k_row0 // tk
    return pl.pallas_call(
        functools.partial(_diff_attn_kernel, nk=nk),
        grid=(heads, nq, nk),
        in_specs=[pl.BlockSpec(memory_space=pltpu.SMEM),
                  pl.BlockSpec((tq, LANES), lambda h, i, j: (qb + i, h)),
                  pl.BlockSpec((tk, LANES), lambda h, i, j: (kb + j, heads + h)),
                  pl.BlockSpec((tk, LANES), lambda h, i, j: (kb + j, 2 * heads + h)),
                  pl.BlockSpec((1, LANES), lambda h, i, j: (0, 0))],
        out_specs=pl.BlockSpec((tq, LANES), lambda h, i, j: (i, h)),
        out_shape=jax.ShapeDtypeStruct((n_q, heads * LANES), BF16),
        scratch_shapes=[pltpu.VMEM((2, tq, LANES), BF16), pltpu.VMEM((2, tq, 1), F32),
                        pltpu.VMEM((2, tq, 1), F32), pltpu.VMEM((2, tq, LANES), F32)],
        compiler_params=_cp("arbitrary", "arbitrary", "arbitrary"),
        name="diff_attn",
    )(lam2, qkv, qkv, qkv, subln_g.reshape(1, LANES))


def _na_bias_tables(rpb, rows):
    wr, wc = NA_WIN_ROWS, NA_WIN_COLS
    r0s = np.array([0, NA_QROWS, rows - NA_QROWS])
    ks = np.clip(r0s - wr // 2, 0, rows - NA_KROWS)
    a = np.arange(NA_QROWS)[None, :, None, None, None]
    c = np.arange(GRID_W)[None, None, :, None, None]
    b = np.arange(NA_KROWS)[None, None, None, :, None]
    kc = np.arange(GRID_W)[None, None, None, None, :]
    r = r0s[:, None, None, None, None] + a
    kr = ks[:, None, None, None, None] + b
    rs = np.clip(r - wr // 2, 0, rows - wr)
    cs = np.clip(c - wc // 2, 0, GRID_W - wc)
    valid = (kr >= rs) & (kr < rs + wr) & (kc >= cs) & (kc < cs + wc)
    shape = (3, NA_QROWS, GRID_W, NA_KROWS, GRID_W)
    valid = np.broadcast_to(valid, shape).reshape(3, NA_QROWS * GRID_W, NA_KROWS * GRID_W)
    ir = np.broadcast_to(np.clip(kr - r + wr - 1, 0, 2 * wr - 2), shape).reshape(valid.shape)
    ic = np.broadcast_to(np.clip(kc - c + wc - 1, 0, 2 * wc - 2), shape).reshape(valid.shape)
    return jnp.where(valid[None], rpb.astype(F32)[:, ir, ic], NEG)


def _na_kernel(q_ref, k_ref, v_ref, b_ref, o_ref, *, n_lat, n_ctx, rows):
    rb = pl.program_id(1)
    n_rb = rows // NA_QROWS
    q = q_ref[...]
    kc = k_ref[n_lat:n_lat + n_ctx, :]
    vc = v_ref[n_lat:n_lat + n_ctx, :]
    nt = (((1,), (1,)), ((), ()))
    s_ctx = lax.dot_general(q, kc, nt, preferred_element_type=F32)

    @pl.when(rb < n_rb)
    def _():
        ks = jnp.clip(rb * NA_QROWS - NA_WIN_ROWS // 2, 0, rows - NA_KROWS)
        start = pl.multiple_of(ks * GRID_W, GRID_W)
        kw = k_ref[pl.ds(start, NA_KROWS * GRID_W), :]
        vw = v_ref[pl.ds(start, NA_KROWS * GRID_W), :]
        s_loc = lax.dot_general(q, kw, nt, preferred_element_type=F32) + b_ref[0, 0]
        m = jnp.maximum(jnp.max(s_loc, axis=1, keepdims=True), jnp.max(s_ctx, axis=1, keepdims=True))
        p_loc = jnp.exp(s_loc - m)
        p_ctx = jnp.exp(s_ctx - m)
        l = jnp.sum(p_loc, axis=1, keepdims=True) + jnp.sum(p_ctx, axis=1, keepdims=True)
        o = (jnp.dot(p_ctx.astype(BF16), vc, preferred_element_type=F32)
             + jnp.dot(p_loc.astype(BF16), vw, preferred_element_type=F32))
        o_ref[...] = (o / l).astype(o_ref.dtype)

    @pl.when(rb == n_rb)
    def _():
        m = jnp.max(s_ctx, axis=1, keepdims=True)
        p = jnp.exp(s_ctx - m)
        l = jnp.sum(p, axis=1, keepdims=True)
        o = jnp.dot(p.astype(BF16), vc, preferred_element_type=F32)
        o_ref[...] = (o / l).astype(o_ref.dtype)


def _na_attn(qkv, bias, n_lat, n_ctx, da_heads, heads):
    t = qkv.shape[0]
    rows = n_lat // GRID_W
    n_rb = rows // NA_QROWS
    tq = NA_QROWS * GRID_W
    assert tq == ROW_TILE and n_ctx == ROW_TILE
    q0, k0, v0 = 3 * da_heads, 3 * da_heads + heads, 3 * da_heads + 2 * heads

    def bias_map(h, rb):
        return (h, jnp.where(rb == 0, 0, jnp.where(rb == n_rb - 1, 2, 1)), 0, 0)

    return pl.pallas_call(
        functools.partial(_na_kernel, n_lat=n_lat, n_ctx=n_ctx, rows=rows),
        grid=(heads, n_rb + 1),
        in_specs=[pl.BlockSpec((tq, LANES), lambda h, rb: (rb, q0 + h)),
                  pl.BlockSpec((t, LANES), lambda h, rb: (0, k0 + h)),
                  pl.BlockSpec((t, LANES), lambda h, rb: (0, v0 + h)),
                  pl.BlockSpec((1, 1, tq, NA_KROWS * GRID_W), bias_map)],
        out_specs=pl.BlockSpec((tq, LANES), lambda h, rb: (rb, h)),
        out_shape=jax.ShapeDtypeStruct((t, heads * LANES), BF16),
        compiler_params=_cp("arbitrary", "arbitrary"),
        name="na_attn",
    )(qkv, qkv, qkv, bias)


def _hgrn_chunk(q, v, z, lb, st, rev):
    c, sub, nsub = HG_CHUNK, HG_SUB, HG_CHUNK // HG_SUB
    nt = (((1,), (1,)), ((), ()))
    f = lb + (1.0 - lb) * jax.nn.sigmoid(z)
    kk = 1.0 - f
    g = jnp.log(f)
    ri = lax.broadcasted_iota(jnp.int32, (c, c), 0)
    ci = lax.broadcasted_iota(jnp.int32, (c, c), 1)
    tri = ((ri <= ci) if rev else (ri >= ci)).astype(F32)
    cum = jnp.dot(tri, g, preferred_element_type=F32, precision=lax.Precision.HIGHEST)
    total = jnp.sum(g, axis=0, keepdims=True)
    qs = _silu(q) * (LANES ** -0.5)
    vb = v.astype(BF16)

    o_inter = lax.dot_general((qs * jnp.exp(cum)).astype(BF16), st.astype(BF16), nt, preferred_element_type=F32)

    def blk(i):
        lo = (nsub - 1 - i) * sub if rev else i * sub
        return lo, lo + sub

    ti = lax.broadcasted_iota(jnp.int32, (sub, LANES), 0)
    ones = jnp.ones((LANES, LANES), BF16)
    o_blocks = [None] * nsub
    for i in range(nsub):
        lo, hi = blk(i)
        cum_i, q_i = cum[lo:hi], qs[lo:hi]
        o_i = o_inter[lo:hi]
        if i > 0:
            brow = hi if rev else lo - 1
            bnd = cum[brow:brow + 1]
            elo, ehi = (hi, c) if rev else (0, lo)
            qt = (q_i * jnp.exp(cum_i - bnd)).astype(BF16)
            kt = (kk[elo:ehi] * jnp.exp(bnd - cum[elo:ehi])).astype(BF16)
            att = lax.dot_general(qt, kt, nt, preferred_element_type=F32)
            o_i = o_i + jnp.dot(att.astype(BF16), vb[elo:ehi], preferred_element_type=F32)
        zs = []
        for s in range(sub):
            valid = (ti <= s) if rev else (ti >= s)
            e = jnp.exp(jnp.where(valid, cum_i - cum[lo + s:lo + s + 1], NEG))
            zs.append(e * q_i * kk[lo + s:lo + s + 1])
        a_all = jnp.dot(jnp.concatenate(zs, axis=0).astype(BF16), ones, preferred_element_type=F32)
        for s in range(sub):
            o_i = o_i + a_all[s * sub:(s + 1) * sub] * v[lo + s:lo + s + 1]
        o_blocks[(nsub - 1 - i) if rev else i] = o_i
    o = jnp.concatenate(o_blocks, axis=0)

    kdec = (kk * jnp.exp(total - cum)).astype(BF16)
    st_new = st * jnp.exp(total) + lax.dot_general(vb, kdec, (((0,), (0,)), ((), ())), preferred_element_type=F32)
    return o, st_new


def _hgrn_kernel(*refs, rev, final, n_chunks):
    if final:
        q_ref, v_ref, z_ref, lb_ref, of_ref, gate_ref, gn_ref, o_ref, st_ref = refs
    else:
        q_ref, v_ref, z_ref, lb_ref, o_ref, st_ref = refs

    @pl.when(pl.program_id(1) == 0)
    def _():
        st_ref[...] = jnp.zeros(st_ref.shape, F32)

    lb = lb_ref[...]
    for step in range(n_chunks):
        ch = (n_chunks - 1 - step) if rev else step
        r0 = ch * HG_CHUNK
        o, st_new = _hgrn_chunk(q_ref[r0:r0 + HG_CHUNK, :], v_ref[r0:r0 + HG_CHUNK, :],
                                z_ref[r0:r0 + HG_CHUNK, :], lb, st_ref[...], rev)
        st_ref[...] = st_new
        if final:
            o = o + of_ref[r0:r0 + HG_CHUNK, :]
            o = _rms(o, gn_ref[...]) * _silu(gate_ref[r0:r0 + HG_CHUNK, :])
        o_ref[r0:r0 + HG_CHUNK, :] = o.astype(o_ref.dtype)


def _hgrn_dir(p, lb_dir, rev, n_lat, heads, o_fwd=None, gnorm_g=None):
    t = p.shape[0]
    d = heads * LANES
    n_blk = t // ROW_TILE
    last = n_blk - 1
    final = o_fwd is not None

    def tok(j):
        return jnp.where(j == 0, last, (last - j) if rev else (j - 1))

    zcol = 3 * heads if rev else 2 * heads
    in_specs = [pl.BlockSpec((ROW_TILE, LANES), lambda h, j: (tok(j), h)),
                pl.BlockSpec((ROW_TILE, LANES), lambda h, j: (tok(j), heads + h)),
                pl.BlockSpec((ROW_TILE, LANES), lambda h, j: (tok(j), zcol + h)),
                pl.BlockSpec((1, LANES), lambda h, j: (0, h))]
    args = [p, p, p, lb_dir.reshape(1, d)]
    if final:
        in_specs += [pl.BlockSpec((ROW_TILE, LANES), lambda h, j: (tok(j), h)),
                     pl.BlockSpec((ROW_TILE, LANES), lambda h, j: (tok(j), 4 * heads + h)),
                     pl.BlockSpec((1, LANES), lambda h, j: (0, 0))]
        args += [o_fwd, p, gnorm_g.reshape(1, LANES)]
    return pl.pallas_call(
        functools.partial(_hgrn_kernel, rev=rev, final=final, n_chunks=ROW_TILE // HG_CHUNK),
        grid=(heads, n_blk),
        in_specs=in_specs,
        out_specs=pl.BlockSpec((ROW_TILE, LANES), lambda h, j: (tok(j), h)),
        out_shape=jax.ShapeDtypeStruct((t, d), BF16 if final else F32),
        scratch_shapes=[pltpu.VMEM((LANES, LANES), F32)],
        compiler_params=_cp("arbitrary", "arbitrary"),
        name="hgrn_bwd" if rev else "hgrn_fwd",
    )(*args)


def _new_expert(be_ref):
    b = pl.program_id(0)
    return jnp.logical_or(b == 0, be_ref[b] != be_ref[jnp.maximum(b - 1, 0)])


def _moe_up_kernel(be_ref, nu_ref, x_ref, wg_ref, wu_ref, o_ref, wgb_ref, wub_ref):
    b = pl.program_id(0)

    @pl.when(jnp.logical_and(b < nu_ref[0], _new_expert(be_ref)))
    def _():
        wgb_ref[...] = wg_ref[0].astype(BF16)
        wub_ref[...] = wu_ref[0].astype(BF16)

    @pl.when(b < nu_ref[0])
    def _():
        x = x_ref[...]
        hg = jnp.dot(x, wgb_ref[...], preferred_element_type=F32)
        hu = jnp.dot(x, wub_ref[...], preferred_element_type=F32)
        o_ref[...] = (_silu(hg) * hu).astype(o_ref.dtype)

    @pl.when(b >= nu_ref[0])
    def _():
        o_ref[...] = jnp.zeros(o_ref.shape, o_ref.dtype)


def _moe_down_kernel(be_ref, nu_ref, h_ref, wd_ref, gt_ref, o_ref, wdb_ref):
    b = pl.program_id(0)

    @pl.when(jnp.logical_and(b < nu_ref[0], _new_expert(be_ref)))
    def _():
        wdb_ref[...] = wd_ref[0].astype(BF16)

    @pl.when(b < nu_ref[0])
    def _():
        y = jnp.dot(h_ref[...], wdb_ref[...], preferred_element_type=F32)
        o_ref[...] = y * gt_ref[...]

    @pl.when(b >= nu_ref[0])
    def _():
        o_ref[...] = jnp.zeros(o_ref.shape, o_ref.dtype)


def _moe_experts(xs, blk_expert, n_used, slot_gate, w_gate, w_up, w_down):
    n_slot, d = xs.shape
    de = w_gate.shape[2]
    n_blk = n_slot // MOE_BM
    hid = pl.pallas_call(
        _moe_up_kernel,
        grid_spec=pltpu.PrefetchScalarGridSpec(
            num_scalar_prefetch=2, grid=(n_blk,),
            in_specs=[pl.BlockSpec((MOE_BM, d), lambda b, be, nu: (b, 0)),
                      pl.BlockSpec((1, d, de), lambda b, be, nu: (be[b], 0, 0)),
                      pl.BlockSpec((1, d, de), lambda b, be, nu: (be[b], 0, 0))],
            out_specs=pl.BlockSpec((MOE_BM, de), lambda b, be, nu: (b, 0)),
            scratch_shapes=[pltpu.VMEM((d, de), BF16), pltpu.VMEM((d, de), BF16)]),
        out_shape=jax.ShapeDtypeStruct((n_slot, de), BF16),
        compiler_params=_cp("arbitrary"),
        name="moe_up",
    )(blk_expert, n_used, xs, w_gate, w_up)
    return pl.pallas_call(
        _moe_down_kernel,
        grid_spec=pltpu.PrefetchScalarGridSpec(
            num_scalar_prefetch=2, grid=(n_blk,),
            in_specs=[pl.BlockSpec((MOE_BM, de), lambda b, be, nu: (b, 0)),
                      pl.BlockSpec((1, de, d), lambda b, be, nu: (be[b], 0, 0)),
                      pl.BlockSpec((MOE_BM, 1), lambda b, be, nu: (b, 0))],
            out_specs=pl.BlockSpec((MOE_BM, d), lambda b, be, nu: (b, 0)),
            scratch_shapes=[pltpu.VMEM((de, d), BF16)]),
        out_shape=jax.ShapeDtypeStruct((n_slot, d), F32),
        compiler_params=_cp("arbitrary"),
        name="moe_down",
    )(blk_expert, n_used, hid, w_down, slot_gate.reshape(n_slot, 1))


def _route(logits):
    n_exp = N_GROUPS * EXPERTS_PER_GROUP
    t = logits.shape[0]
    g_logits = logits[:, :N_GROUPS]
    g_prob = jax.nn.softmax(g_logits, axis=-1)
    g_sel = jnp.argmax(g_logits, axis=-1).astype(jnp.int32)
    g_w = jnp.take_along_axis(g_prob, g_sel[:, None], axis=1)[:, 0]
    e_logits = logits[:, N_GROUPS:N_GROUPS + n_exp].reshape(t, N_GROUPS, EXPERTS_PER_GROUP)
    e_logits = jnp.take_along_axis(e_logits, g_sel[:, None, None], axis=1)[:, 0]
    top_v, top_i = lax.top_k(e_logits, TOP_K)
    weights = g_w[:, None] * jax.nn.softmax(top_v, axis=-1)
    expert = g_sel[:, None] * EXPERTS_PER_GROUP + top_i.astype(jnp.int32)
    return expert, weights


def _dispatch(expert, weights):
    n_exp = N_GROUPS * EXPERTS_PER_GROUP
    t = expert.shape[0]
    n_as = t * TOP_K
    eid = expert.reshape(-1)
    gate = weights.reshape(-1)
    tok = jnp.repeat(jnp.arange(t, dtype=jnp.int32), TOP_K)
    order = jnp.argsort(eid).astype(jnp.int32)
    eid_s = eid[order]
    counts = jnp.zeros((n_exp,), jnp.int32).at[eid].add(1)
    padded = (counts + MOE_BM - 1) // MOE_BM * MOE_BM
    start = jnp.cumsum(counts) - counts
    pend = jnp.cumsum(padded)
    pstart = pend - padded
    dest = pstart[eid_s] + jnp.arange(n_as, dtype=jnp.int32) - start[eid_s]
    n_blk = -(-n_as // MOE_BM) + n_exp
    n_slot = n_blk * MOE_BM
    slot_src = jnp.full((n_slot,), n_as, jnp.int32).at[dest].set(order)
    slot_tok = jnp.concatenate([tok, jnp.zeros((1,), jnp.int32)])[slot_src]
    slot_gate = jnp.concatenate([gate, jnp.zeros((1,), gate.dtype)])[slot_src]
    blk_expert = jnp.searchsorted(pend, jnp.arange(n_blk, dtype=jnp.int32) * MOE_BM, side='right')
    blk_expert = jnp.minimum(blk_expert, n_exp - 1).astype(jnp.int32)
    n_used = (pend[-1] // MOE_BM).astype(jnp.int32).reshape(1)
    slot_of = jnp.zeros((n_as,), jnp.int32).at[order].set(dest).reshape(t, TOP_K)
    return slot_tok, slot_gate, blk_expert, n_used, slot_of


def _moe(m, logits, w_gate, w_up, w_down):
    expert, weights = _route(logits)
    slot_tok, slot_gate, blk_expert, n_used, slot_of = _dispatch(expert, weights)
    xs = m[slot_tok]
    ys = _moe_experts(xs, blk_expert, n_used, slot_gate, w_gate, w_up, w_down)
    return ys[slot_of[:, 0]] + ys[slot_of[:, 1]]


def _diff_lambda(lam_p, layer_idx):
    lam_init = 0.8 - 0.6 * math.exp(-0.3 * layer_idx)
    lf = lam_p.astype(F32)
    lam = jnp.exp(jnp.sum(lf[0] * lf[1])) - jnp.exp(jnp.sum(lf[2] * lf[3])) + lam_init
    return jnp.stack([lam, jnp.asarray(1.0 - lam_init, F32)]).astype(F32)


def kernel(x, c, ctx, c_ctx, w_mod, b_mod, norm1_g, norm2_g, att_w_in, att_w_out, att_lambda, att_subln_g,
           att_rpb, rec_w_in, rec_w_out, rec_lb_logits, rec_gnorm_g, moe_w_group, moe_b_group, moe_w_router,
           moe_b_router, moe_w_gate, moe_w_up, moe_w_down, final_norm_g):
    b_, n_lat, d = x.shape
    n_ctx = ctx.shape[1]
    assert b_ == 1 and n_ctx == ROW_TILE and n_lat % ROW_TILE == 0 and d % LANES == 0
    depth = w_mod.shape[0]
    daw, naw = DA_HEADS * LANES, NA_HEADS * LANES
    n_exp = N_GROUPS * EXPERTS_PER_GROUP

    lbp = jax.nn.softmax(rec_lb_logits.astype(F32), axis=0)
    lbs = jnp.cumsum(lbp, axis=0) - lbp[0:1]

    cvec = jnp.zeros((8, d), F32).at[0].set(c[0]).at[1].set(c_ctx)
    mods = _modulation(cvec, w_mod, b_mod)
    cos, sin = _rope_tables(n_lat, n_ctx)

    h = jnp.concatenate([x[0], ctx[0]], axis=0)
    a = _ln_mod(h, norm1_g[0], mods[0], 0, 1, n_lat)
    out = None
    for l in range(depth):
        j = l // 2
        mod = mods[l]
        if l % 2 == 0:
            p = _matmul(a, att_w_in[j], F32)
            qkv = _attn_prep(p, cos, sin, daw, naw)
            lam2 = _diff_lambda(att_lambda[j], l)
            da_lat = _diff_attn(qkv, lam2, att_subln_g[j], 0, n_lat, 0, n_lat + n_ctx, DA_HEADS)
            da_ctx = _diff_attn(qkv, lam2, att_subln_g[j], n_lat, n_ctx, n_lat, n_ctx, DA_HEADS)
            bias = _na_bias_tables(att_rpb[j], n_lat // GRID_W)
            na = _na_attn(qkv, bias, n_lat, n_ctx, DA_HEADS, NA_HEADS)
            cat = jnp.concatenate([jnp.concatenate([da_lat, da_ctx], axis=0), na], axis=1)
            y = _matmul(cat, att_w_out[j], F32)
        else:
            p = _matmul(a, rec_w_in[j], F32)
            o_f = _hgrn_dir(p, lbs[l, 0], False, n_lat, HG_HEADS)
            o = _hgrn_dir(p, lbs[l, 1], True, n_lat, HG_HEADS, o_fwd=o_f, gnorm_g=rec_gnorm_g[j])
            y = _matmul(o, rec_w_out[j], F32)
        w_route = jnp.zeros((d, LANES), F32).at[:, :N_GROUPS].set(moe_w_group[l])
        w_route = w_route.at[:, N_GROUPS:N_GROUPS + n_exp].set(moe_w_router[l])
        b_route = jnp.zeros((1, LANES), F32).at[0, :N_GROUPS].set(moe_b_group[l])
        b_route = b_route.at[0, N_GROUPS:N_GROUPS + n_exp].set(moe_b_router[l])
        h, m, logits = _res_ln_mod_route(h, y, norm2_g[l], mod, 2, 3, 4, w_route, b_route, n_lat)
        moe_out = _moe(m, logits, moe_w_gate[l], moe_w_up[l], moe_w_down[l])
        if l + 1 < depth:
            h, a = _res_ln_mod(h, moe_out, norm1_g[l + 1], mod, 5, mods[l + 1], 0, 1, n_lat)
        else:
            out = _res_final(h, moe_out, final_norm_g, mod, 5, n_lat)
    return out[None]
```

```python
import functools
import math

import numpy as np
import jax
import jax.numpy as jnp
from jax import lax
from jax.experimental import pallas as pl
from jax.experimental.pallas import tpu as pltpu

F32 = jnp.float32
BF16 = jnp.bfloat16

GRID_W = 64
EPS = 1e-6
DA_HEADS = 8
DA_QK_DIM = 64
NA_HEADS = 8
NA_WIN_ROWS = 8
NA_WIN_COLS = 16
ROPE_THETA = 10000.0
HG_HEADS = 16
N_GROUPS = 4
EXPERTS_PER_GROUP = 8
TOP_K = 2

LANES = 128
ROW_TILE = 256
HG_CHUNK = 64
HG_SUB = 16
NA_QROWS = 4
NA_KROWS = 12
MOE_BM = 256
NEG = -1e30
VMEM_LIMIT = 56 << 20


def _cp(*sem):
    return pltpu.CompilerParams(dimension_semantics=sem, vmem_limit_bytes=VMEM_LIMIT)


def _pick(n, cands):
    for c in cands:
        if n % c == 0:
            return c
    raise ValueError(f"no tile for {n} in {cands}")


def _silu(x):
    return x * jax.nn.sigmoid(x)


def _rms(x, g):
    return x * lax.rsqrt(jnp.mean(x * x, axis=-1, keepdims=True) + EPS) * g


def _mod_kernel(c_ref, w_ref, b_ref, o_ref):
    s = _silu(c_ref[...]).astype(BF16)
    o_ref[0] = jnp.dot(s, w_ref[0].astype(BF16), preferred_element_type=F32) + b_ref[0]


def _modulation(cvec, w_mod, b_mod):
    depth, d, n = w_mod.shape
    tn = _pick(n, (1024, 512, 256, 128))
    return pl.pallas_call(
        _mod_kernel,
        grid=(depth, n // tn),
        in_specs=[pl.BlockSpec((8, d), lambda l, j: (0, 0)),
                  pl.BlockSpec((1, d, tn), lambda l, j: (l, 0, j)),
                  pl.BlockSpec((1, 1, tn), lambda l, j: (l, 0, j))],
        out_specs=pl.BlockSpec((1, 8, tn), lambda l, j: (l, 0, j)),
        out_shape=jax.ShapeDtypeStruct((depth, 8, n), F32),
        compiler_params=_cp("arbitrary", "arbitrary"),
        name="modulation",
    )(cvec, w_mod, b_mod.reshape(depth, 1, n))


def _mod_row(ref, n_lat_tiles):
    r = (pl.program_id(0) >= n_lat_tiles).astype(jnp.int32)
    return ref[pl.ds(r, 1), :]


def _ln_mod_kernel(h_ref, g_ref, sh_ref, sc_ref, a_ref, *, n_lat_tiles):
    y = _rms(h_ref[...], g_ref[...])
    a = y * (1.0 + _mod_row(sc_ref, n_lat_tiles)) + _mod_row(sh_ref, n_lat_tiles)
    a_ref[...] = a.astype(a_ref.dtype)


def _row_gather(idx_ref, base, src_hbm, dst, sem, n_rows, start):
    def body(r, carry):
        cp = pltpu.make_async_copy(src_hbm.at[pl.ds(idx_ref[base + r], 1)], dst.at[pl.ds(r, 1)], sem)
        if start:
            cp.start()
        else:
            cp.wait()
        return carry

    lax.fori_loop(0, n_rows, body, 0, unroll=8)


def _combined_expert_rows(idx_ref, ys_hbm, ybuf, sem, t):
    i = pl.program_id(0)

    def gather(tile, start):
        slot = tile % 2
        for k in range(TOP_K):
            _row_gather(idx_ref, k * t + tile * ROW_TILE, ys_hbm, ybuf.at[slot, k], sem.at[slot], ROW_TILE, start)

    @pl.when(i == 0)
    def _():
        gather(i, True)

    @pl.when(i + 1 < pl.num_programs(0))
    def _():
        gather(i + 1, True)

    gather(i, False)
    y = ybuf[i % 2, 0]
    for k in range(1, TOP_K):
        y = y + ybuf[i % 2, k]
    return y


def _res_ln_mod_kernel(idx_ref, h_ref, ys_hbm, gt_ref, g_ref, sh_ref, sc_ref, ho_ref, a_ref, ybuf, sem,
                       *, n_lat_tiles, t):
    y = _combined_expert_rows(idx_ref, ys_hbm, ybuf, sem, t)
    h = h_ref[...] + _mod_row(gt_ref, n_lat_tiles) * y
    ho_ref[...] = h
    a = _rms(h, g_ref[...]) * (1.0 + _mod_row(sc_ref, n_lat_tiles)) + _mod_row(sh_ref, n_lat_tiles)
    a_ref[...] = a.astype(a_ref.dtype)


def _res_ln_mod_route_kernel(h_ref, y_ref, gt_ref, g_ref, sh_ref, sc_ref, wr_ref, br_ref,
                             ho_ref, a_ref, lg_ref, *, n_lat_tiles):
    h = h_ref[...] + _mod_row(gt_ref, n_lat_tiles) * y_ref[...]
    ho_ref[...] = h
    a = _rms(h, g_ref[...]) * (1.0 + _mod_row(sc_ref, n_lat_tiles)) + _mod_row(sh_ref, n_lat_tiles)
    a_ref[...] = a.astype(a_ref.dtype)
    lg_ref[...] = jnp.dot(a, wr_ref[...], preferred_element_type=F32,
                          precision=lax.Precision.HIGHEST) + br_ref[...]


def _res_final_kernel(idx_ref, h_ref, ys_hbm, gt_ref, g_ref, o_ref, ybuf, sem, *, t):
    y = _combined_expert_rows(idx_ref, ys_hbm, ybuf, sem, t)
    h = h_ref[...] + gt_ref[pl.ds(0, 1), :] * y
    o_ref[...] = _rms(h, g_ref[...])


def _row_spec(d):
    return pl.BlockSpec((ROW_TILE, d), lambda i, *_: (i, 0))


def _vec_spec(d):
    return pl.BlockSpec((1, d), lambda i, *_: (0, 0))


def _mod_spec(d, k):
    return pl.BlockSpec((8, d), lambda i, *_: (0, k))


def _combine_scratch(d):
    return [pltpu.VMEM((2, TOP_K, ROW_TILE, d), F32), pltpu.SemaphoreType.DMA((2,))]


def _ln_mod(h, g, mod, k_shift, k_scale, n_lat):
    t, d = h.shape
    return pl.pallas_call(
        functools.partial(_ln_mod_kernel, n_lat_tiles=n_lat // ROW_TILE),
        grid=(t // ROW_TILE,),
        in_specs=[_row_spec(d), _vec_spec(d), _mod_spec(d, k_shift), _mod_spec(d, k_scale)],
        out_specs=_row_spec(d),
        out_shape=jax.ShapeDtypeStruct((t, d), BF16),
        compiler_params=_cp("arbitrary"),
        name="ln_mod",
    )(h, g.reshape(1, d), mod, mod)


def _res_ln_mod(h, ys, slot_idx, g, mod_gate, k_gate, mod_next, k_shift, k_scale, n_lat):
    t, d = h.shape
    return pl.pallas_call(
        functools.partial(_res_ln_mod_kernel, n_lat_tiles=n_lat // ROW_TILE, t=t),
        grid_spec=pltpu.PrefetchScalarGridSpec(
            num_scalar_prefetch=1, grid=(t // ROW_TILE,),
            in_specs=[_row_spec(d), pl.BlockSpec(memory_space=pl.ANY), _mod_spec(d, k_gate), _vec_spec(d),
                      _mod_spec(d, k_shift), _mod_spec(d, k_scale)],
            out_specs=[_row_spec(d), _row_spec(d)],
            scratch_shapes=_combine_scratch(d)),
        out_shape=[jax.ShapeDtypeStruct((t, d), F32), jax.ShapeDtypeStruct((t, d), BF16)],
        compiler_params=_cp("arbitrary"),
        name="res_ln_mod",
    )(slot_idx, h, ys, mod_gate, g.reshape(1, d), mod_next, mod_next)


def _res_ln_mod_route(h, y, g, mod, k_gate, k_shift, k_scale, w_route, b_route, n_lat):
    t, d = h.shape
    return pl.pallas_call(
        functools.partial(_res_ln_mod_route_kernel, n_lat_tiles=n_lat // ROW_TILE),
        grid=(t // ROW_TILE,),
        in_specs=[_row_spec(d), _row_spec(d), _mod_spec(d, k_gate), _vec_spec(d),
                  _mod_spec(d, k_shift), _mod_spec(d, k_scale),
                  pl.BlockSpec((d, LANES), lambda i: (0, 0)), _vec_spec(LANES)],
        out_specs=[_row_spec(d), _row_spec(d), _row_spec(LANES)],
        out_shape=[jax.ShapeDtypeStruct((t, d), F32), jax.ShapeDtypeStruct((t, d), F32),
                   jax.ShapeDtypeStruct((t, LANES), F32)],
        compiler_params=_cp("arbitrary"),
        name="res_ln_mod_route",
    )(h, y, mod, g.reshape(1, d), mod, mod, w_route, b_route)


def _res_final(h, ys, slot_idx, g, mod, k_gate, n_lat):
    t, d = h.shape
    return pl.pallas_call(
        functools.partial(_res_final_kernel, t=t),
        grid_spec=pltpu.PrefetchScalarGridSpec(
            num_scalar_prefetch=1, grid=(n_lat // ROW_TILE,),
            in_specs=[_row_spec(d), pl.BlockSpec(memory_space=pl.ANY), _mod_spec(d, k_gate), _vec_spec(d)],
            out_specs=_row_spec(d),
            scratch_shapes=_combine_scratch(d)),
        out_shape=jax.ShapeDtypeStruct((n_lat, d), F32),
        compiler_params=_cp("arbitrary"),
        name="res_final",
    )(slot_idx, h, ys, mod, g.reshape(1, d))


def _matmul_kernel(a_ref, w_ref, o_ref, wb_ref):
    @pl.when(pl.program_id(1) == 0)
    def _():
        wb_ref[...] = w_ref[0].astype(BF16)

    o_ref[...] = jnp.dot(a_ref[...], wb_ref[...], preferred_element_type=F32).astype(o_ref.dtype)


def _matmul(a, w_stack, layer, out_dtype):
    m, k = a.shape
    n = w_stack.shape[2]
    tm = _pick(m, (768, 640, 512, 256))
    tn = _pick(n, (1024, 512, 256, 128))
    return pl.pallas_call(
        _matmul_kernel,
        grid=(n // tn, m // tm),
        in_specs=[pl.BlockSpec((tm, k), lambda j, i: (i, 0)),
                  pl.BlockSpec((1, k, tn), lambda j, i: (layer, 0, j))],
        out_specs=pl.BlockSpec((tm, tn), lambda j, i: (i, j)),
        out_shape=jax.ShapeDtypeStruct((m, n), out_dtype),
        scratch_shapes=[pltpu.VMEM((k, tn), BF16)],
        compiler_params=_cp("arbitrary", "arbitrary"),
        name="matmul",
    )(a, w_stack)


def _attn_prep_kernel(p_ref, cos_ref, sin_ref, o_ref, *, daw, naw, sa, sn):
    cos = cos_ref[...]
    sin = sin_ref[...]
    lane = lax.broadcasted_iota(jnp.int32, cos.shape, 1)
    first_half = (lane % (DA_QK_DIM // 2)) < (DA_QK_DIM // 4)
    quarter = DA_QK_DIM // 4
    for j in range((3 * daw + 3 * naw) // LANES):
        c0 = j * LANES
        x = p_ref[:, c0:c0 + LANES]
        if c0 < 2 * daw:
            partner = jnp.where(first_half, pltpu.roll(x, LANES - quarter, 1), pltpu.roll(x, quarter, 1))
            x = x * cos + partner * sin
            if c0 < daw:
                x = x * sa
        elif 3 * daw <= c0 < 3 * daw + naw:
            x = x * sn
        o_ref[:, c0:c0 + LANES] = x.astype(o_ref.dtype)


def _rope_tables(n_lat, n_ctx):
    t = jnp.arange(n_lat, dtype=jnp.int32)
    rows = (t // GRID_W).astype(F32)
    cols = (t % GRID_W).astype(F32)
    n_freq = DA_QK_DIM // 4
    inv_freq = ROPE_THETA ** (-jnp.arange(n_freq, dtype=F32) / n_freq)
    ang_r = rows[:, None] * inv_freq
    ang_c = cols[:, None] * inv_freq
    ang = jnp.concatenate([ang_r, ang_r, ang_c, ang_c], axis=1)
    sign = jnp.concatenate([-jnp.ones((n_freq,), F32), jnp.ones((n_freq,), F32)] * 2)
    cos = jnp.cos(ang)
    sin = jnp.sin(ang) * sign
    reps = LANES // DA_QK_DIM
    cos = jnp.concatenate([jnp.tile(cos, (1, reps)), jnp.ones((n_ctx, LANES), F32)], axis=0)
    sin = jnp.concatenate([jnp.tile(sin, (1, reps)), jnp.zeros((n_ctx, LANES), F32)], axis=0)
    return cos, sin


def _attn_prep(p, cos, sin, daw, naw):
    t, n = p.shape
    kern = functools.partial(_attn_prep_kernel, daw=daw, naw=naw, sa=DA_QK_DIM ** -0.5, sn=LANES ** -0.5)
    return pl.pallas_call(
        kern,
        grid=(t // ROW_TILE,),
        in_specs=[_row_spec(n), _row_spec(LANES), _row_spec(LANES)],
        out_specs=_row_spec(n),
        out_shape=jax.ShapeDtypeStruct((t, n), BF16),
        compiler_params=_cp("arbitrary"),
        name="attn_prep",
    )(p, cos, sin)


def _diff_attn_kernel(lam_ref, q_ref, k_ref, v_ref, g_ref, o_ref, qm_ref, m_ref, l_ref, acc_ref, *, nk):
    kv = pl.program_id(2)

    @pl.when(kv == 0)
    def _():
        q = q_ref[...]
        lane = lax.broadcasted_iota(jnp.int32, q.shape, 1)
        zero = jnp.zeros_like(q)
        qm_ref[0] = jnp.where(lane < DA_QK_DIM, q, zero)
        qm_ref[1] = jnp.where(lane >= DA_QK_DIM, q, zero)
        m_ref[...] = jnp.full(m_ref.shape, NEG, F32)
        l_ref[...] = jnp.zeros(l_ref.shape, F32)
        acc_ref[...] = jnp.zeros(acc_ref.shape, F32)

    k = k_ref[...]
    v = v_ref[...]
    for mi in range(2):
        s = lax.dot_general(qm_ref[mi], k, (((1,), (1,)), ((), ())), preferred_element_type=F32)
        m_prev = m_ref[mi]
        m_new = jnp.maximum(m_prev, jnp.max(s, axis=1, keepdims=True))
        alpha = jnp.exp(m_prev - m_new)
        p = jnp.exp(s - m_new)
        l_ref[mi] = alpha * l_ref[mi] + jnp.sum(p, axis=1, keepdims=True)
        acc_ref[mi] = alpha * acc_ref[mi] + jnp.dot(p.astype(BF16), v, preferred_element_type=F32)
        m_ref[mi] = m_new

    @pl.when(kv == nk - 1)
    def _():
        o = acc_ref[0] / l_ref[0] - lam_ref[0] * (acc_ref[1] / l_ref[1])
        o_ref[...] = (_rms(o, g_ref[...]) * lam_ref[1]).astype(o_ref.dtype)


def _diff_attn(qkv, lam2, subln_g, q_row0, n_q, k_row0, n_k, heads):
    tq = _pick(n_q, (1024, 512, 256))
    tk = _pick(n_k, (1408, 1280, 1024, 768, 512, 256))
    assert q_row0 % tq == 0 and k_row0 % tk == 0
    nq, nk = n_q // tq, n_k // tk
    qb, kb = q_row0 // tq, k_row0 // tk
    return pl.pallas_call(
        functools.partial(_diff_attn_kernel, nk=nk),
        grid=(heads, nq, nk),
        in_specs=[pl.BlockSpec(memory_space=pltpu.SMEM),
                  pl.BlockSpec((tq, LANES), lambda h, i, j: (qb + i, h)),
                  pl.BlockSpec((tk, LANES), lambda h, i, j: (kb + j, heads + h)),
                  pl.BlockSpec((tk, LANES), lambda h, i, j: (kb + j, 2 * heads + h)),
                  pl.BlockSpec((1, LANES), lambda h, i, j: (0, 0))],
        out_specs=pl.BlockSpec((tq, LANES), lambda h, i, j: (i, h)),
        out_shape=jax.ShapeDtypeStruct((n_q, heads * LANES), BF16),
        scratch_shapes=[pltpu.VMEM((2, tq, LANES), BF16), pltpu.VMEM((2, tq, 1), F32),
                        pltpu.VMEM((2, tq, 1), F32), pltpu.VMEM((2, tq, LANES), F32)],
        compiler_params=_cp("arbitrary", "arbitrary", "arbitrary"),
        name="diff_attn",
    )(lam2, qkv, qkv, qkv, subln_g.reshape(1, LANES))


def _na_bias_tables(rpb, rows):
    wr, wc = NA_WIN_ROWS, NA_WIN_COLS
    heads = rpb.shape[0]
    c = np.arange(GRID_W)[:, None]
    kc = np.arange(GRID_W)[None, :]
    cs = np.clip(c - wc // 2, 0, GRID_W - wc)
    cvalid = (kc >= cs) & (kc < cs + wc)
    onehot = ((kc - c + wc - 1)[None] == np.arange(2 * wc - 1)[:, None, None]) & cvalid[None]
    sel = jnp.asarray(onehot.reshape(2 * wc - 1, GRID_W * GRID_W), F32)
    toe = jnp.dot(rpb.astype(F32).reshape(heads * (2 * wr - 1), 2 * wc - 1), sel,
                  precision=lax.Precision.HIGHEST)
    toe = toe.reshape(heads, 2 * wr - 1, GRID_W, GRID_W) + jnp.asarray(np.where(cvalid, 0.0, NEG), F32)
    toe = jnp.concatenate([toe, jnp.full((heads, 1, GRID_W, GRID_W), NEG, F32)], axis=1)
    r0s = np.array([0, NA_QROWS, rows - NA_QROWS])
    ks = np.clip(r0s - wr // 2, 0, rows - NA_KROWS)
    r = r0s[:, None, None] + np.arange(NA_QROWS)[None, :, None]
    kr = ks[:, None, None] + np.arange(NA_KROWS)[None, None, :]
    rs = np.clip(r - wr // 2, 0, rows - wr)
    idx = np.where((kr >= rs) & (kr < rs + wr), kr - r + wr - 1, 2 * wr - 1)
    blocks = toe[:, idx]
    return blocks.transpose(0, 1, 2, 4, 3, 5).reshape(heads, 3, NA_QROWS * GRID_W, NA_KROWS * GRID_W)


def _na_kernel(q_ref, k_ref, v_ref, b_ref, o_ref, *, n_lat, n_ctx, rows):
    rb = pl.program_id(1)
    n_rb = rows // NA_QROWS
    q = q_ref[...]
    kc = k_ref[n_lat:n_lat + n_ctx, :]
    vc = v_ref[n_lat:n_lat + n_ctx, :]
    nt = (((1,), (1,)), ((), ()))
    s_ctx = lax.dot_general(q, kc, nt, preferred_element_type=F32)

    @pl.when(rb < n_rb)
    def _():
        ks = jnp.clip(rb * NA_QROWS - NA_WIN_ROWS // 2, 0, rows - NA_KROWS)
        start = pl.multiple_of(ks * GRID_W, GRID_W)
        kw = k_ref[pl.ds(start, NA_KROWS * GRID_W), :]
        vw = v_ref[pl.ds(start, NA_KROWS * GRID_W), :]
        s_loc = lax.dot_general(q, kw, nt, preferred_element_type=F32) + b_ref[0, 0]
        m = jnp.maximum(jnp.max(s_loc, axis=1, keepdims=True), jnp.max(s_ctx, axis=1, keepdims=True))
        p_loc = jnp.exp(s_loc - m)
        p_ctx = jnp.exp(s_ctx - m)
        l = jnp.sum(p_loc, axis=1, keepdims=True) + jnp.sum(p_ctx, axis=1, keepdims=True)
        o = (jnp.dot(p_ctx.astype(BF16), vc, preferred_element_type=F32)
             + jnp.dot(p_loc.astype(BF16), vw, preferred_element_type=F32))
        o_ref[...] = (o / l).astype(o_ref.dtype)

    @pl.when(rb == n_rb)
    def _():
        m = jnp.max(s_ctx, axis=1, keepdims=True)
        p = jnp.exp(s_ctx - m)
        l = jnp.sum(p, axis=1, keepdims=True)
        o = jnp.dot(p.astype(BF16), vc, preferred_element_type=F32)
        o_ref[...] = (o / l).astype(o_ref.dtype)


def _na_attn(qkv, bias, n_lat, n_ctx, da_heads, heads):
    t = qkv.shape[0]
    rows = n_lat // GRID_W
    n_rb = rows // NA_QROWS
    tq = NA_QROWS * GRID_W
    assert tq == ROW_TILE and n_ctx == ROW_TILE
    q0, k0, v0 = 3 * da_heads, 3 * da_heads + heads, 3 * da_heads + 2 * heads

    def bias_map(h, rb):
        return (h, jnp.where(rb == 0, 0, jnp.where(rb == n_rb - 1, 2, 1)), 0, 0)

    return pl.pallas_call(
        functools.partial(_na_kernel, n_lat=n_lat, n_ctx=n_ctx, rows=rows),
        grid=(heads, n_rb + 1),
        in_specs=[pl.BlockSpec((tq, LANES), lambda h, rb: (rb, q0 + h)),
                  pl.BlockSpec((t, LANES), lambda h, rb: (0, k0 + h)),
                  pl.BlockSpec((t, LANES), lambda h, rb: (0, v0 + h)),
                  pl.BlockSpec((1, 1, tq, NA_KROWS * GRID_W), bias_map)],
        out_specs=pl.BlockSpec((tq, LANES), lambda h, rb: (rb, h)),
        out_shape=jax.ShapeDtypeStruct((t, heads * LANES), BF16),
        compiler_params=_cp("arbitrary", "arbitrary"),
        name="na_attn",
    )(qkv, qkv, qkv, bias)


def _hgrn_chunk(q, v, z, lb, st, rev):
    c, sub, nsub = HG_CHUNK, HG_SUB, HG_CHUNK // HG_SUB
    nt = (((1,), (1,)), ((), ()))
    f = lb + (1.0 - lb) * jax.nn.sigmoid(z)
    kk = 1.0 - f
    g = jnp.log(f)
    ri = lax.broadcasted_iota(jnp.int32, (c, c), 0)
    ci = lax.broadcasted_iota(jnp.int32, (c, c), 1)
    tri = ((ri <= ci) if rev else (ri >= ci)).astype(F32)
    cum = jnp.dot(tri, g, preferred_element_type=F32, precision=lax.Precision.HIGHEST)
    total = jnp.sum(g, axis=0, keepdims=True)
    qs = _silu(q) * (LANES ** -0.5)
    vb = v.astype(BF16)

    o_inter = lax.dot_general((qs * jnp.exp(cum)).astype(BF16), st.astype(BF16), nt, preferred_element_type=F32)

    def blk(i):
        lo = (nsub - 1 - i) * sub if rev else i * sub
        return lo, lo + sub

    ti = lax.broadcasted_iota(jnp.int32, (sub, LANES), 0)
    ones = jnp.ones((LANES, LANES), BF16)
    o_blocks = [None] * nsub
    for i in range(nsub):
        lo, hi = blk(i)
        cum_i, q_i = cum[lo:hi], qs[lo:hi]
        o_i = o_inter[lo:hi]
        if i > 0:
            brow = hi if rev else lo - 1
            bnd = cum[brow:brow + 1]
            elo, ehi = (hi, c) if rev else (0, lo)
            qt = (q_i * jnp.exp(cum_i - bnd)).astype(BF16)
            kt = (kk[elo:ehi] * jnp.exp(bnd - cum[elo:ehi])).astype(BF16)
            att = lax.dot_general(qt, kt, nt, preferred_element_type=F32)
            o_i = o_i + jnp.dot(att.astype(BF16), vb[elo:ehi], preferred_element_type=F32)
        zs = []
        for s in range(sub):
            valid = (ti <= s) if rev else (ti >= s)
            e = jnp.exp(jnp.where(valid, cum_i - cum[lo + s:lo + s + 1], NEG))
            zs.append(e * q_i * kk[lo + s:lo + s + 1])
        a_all = jnp.dot(jnp.concatenate(zs, axis=0).astype(BF16), ones, preferred_element_type=F32)
        for s in range(sub):
            o_i = o_i + a_all[s * sub:(s + 1) * sub] * v[lo + s:lo + s + 1]
        o_blocks[(nsub - 1 - i) if rev else i] = o_i
    o = jnp.concatenate(o_blocks, axis=0)

    kdec = (kk * jnp.exp(total - cum)).astype(BF16)
    st_new = st * jnp.exp(total) + lax.dot_general(vb, kdec, (((0,), (0,)), ((), ())), preferred_element_type=F32)
    return o, st_new


def _hgrn_kernel(*refs, rev, final, n_chunks):
    if final:
        q_ref, v_ref, z_ref, lb_ref, of_ref, gate_ref, gn_ref, o_ref, st_ref = refs
    else:
        q_ref, v_ref, z_ref, lb_ref, o_ref, st_ref = refs

    @pl.when(pl.program_id(1) == 0)
    def _():
        st_ref[...] = jnp.zeros(st_ref.shape, F32)

    lb = lb_ref[...]
    for step in range(n_chunks):
        ch = (n_chunks - 1 - step) if rev else step
        r0 = ch * HG_CHUNK
        o, st_new = _hgrn_chunk(q_ref[r0:r0 + HG_CHUNK, :], v_ref[r0:r0 + HG_CHUNK, :],
                                z_ref[r0:r0 + HG_CHUNK, :], lb, st_ref[...], rev)
        st_ref[...] = st_new
        if final:
            o = o + of_ref[r0:r0 + HG_CHUNK, :]
            o = _rms(o, gn_ref[...]) * _silu(gate_ref[r0:r0 + HG_CHUNK, :])
        o_ref[r0:r0 + HG_CHUNK, :] = o.astype(o_ref.dtype)


def _hgrn_dir(p, lb_dir, rev, n_lat, heads, o_fwd=None, gnorm_g=None):
    t = p.shape[0]
    d = heads * LANES
    n_blk = t // ROW_TILE
    last = n_blk - 1
    final = o_fwd is not None

    def tok(j):
        return jnp.where(j == 0, last, (last - j) if rev else (j - 1))

    zcol = 3 * heads if rev else 2 * heads
    in_specs = [pl.BlockSpec((ROW_TILE, LANES), lambda h, j: (tok(j), h)),
                pl.BlockSpec((ROW_TILE, LANES), lambda h, j: (tok(j), heads + h)),
                pl.BlockSpec((ROW_TILE, LANES), lambda h, j: (tok(j), zcol + h)),
                pl.BlockSpec((1, LANES), lambda h, j: (0, h))]
    args = [p, p, p, lb_dir.reshape(1, d)]
    if final:
        in_specs += [pl.BlockSpec((ROW_TILE, LANES), lambda h, j: (tok(j), h)),
                     pl.BlockSpec((ROW_TILE, LANES), lambda h, j: (tok(j), 4 * heads + h)),
                     pl.BlockSpec((1, LANES), lambda h, j: (0, 0))]
        args += [o_fwd, p, gnorm_g.reshape(1, LANES)]
    return pl.pallas_call(
        functools.partial(_hgrn_kernel, rev=rev, final=final, n_chunks=ROW_TILE // HG_CHUNK),
        grid=(heads, n_blk),
        in_specs=in_specs,
        out_specs=pl.BlockSpec((ROW_TILE, LANES), lambda h, j: (tok(j), h)),
        out_shape=jax.ShapeDtypeStruct((t, d), BF16 if final else F32),
        scratch_shapes=[pltpu.VMEM((LANES, LANES), F32)],
        compiler_params=_cp("arbitrary", "arbitrary"),
        name="hgrn_bwd" if rev else "hgrn_fwd",
    )(*args)


def _new_expert(be_ref):
    b = pl.program_id(0)
    return jnp.logical_or(b == 0, be_ref[b] != be_ref[jnp.maximum(b - 1, 0)])


def _moe_up_kernel(be_ref, nu_ref, tok_ref, x_hbm, wg_ref, wu_ref, o_ref, wgb_ref, wub_ref, xbuf, sem):
    b = pl.program_id(0)

    def gather(blk, start):
        _row_gather(tok_ref, blk * MOE_BM, x_hbm, xbuf.at[blk % 2], sem.at[blk % 2], MOE_BM, start)

    @pl.when(b == 0)
    def _():
        gather(b, True)

    @pl.when(b + 1 < nu_ref[0])
    def _():
        gather(b + 1, True)

    @pl.when(jnp.logical_and(b < nu_ref[0], _new_expert(be_ref)))
    def _():
        wgb_ref[...] = wg_ref[0, 0].astype(BF16)
        wub_ref[...] = wu_ref[0, 0].astype(BF16)

    @pl.when(b < nu_ref[0])
    def _():
        gather(b, False)
        x = xbuf[b % 2].astype(BF16)
        hg = jnp.dot(x, wgb_ref[...], preferred_element_type=F32)
        hu = jnp.dot(x, wub_ref[...], preferred_element_type=F32)
        o_ref[...] = (_silu(hg) * hu).astype(o_ref.dtype)

    @pl.when(b >= nu_ref[0])
    def _():
        o_ref[...] = jnp.zeros(o_ref.shape, o_ref.dtype)


def _moe_down_kernel(be_ref, nu_ref, h_ref, wd_ref, gt_ref, o_ref, wdb_ref):
    b = pl.program_id(0)

    @pl.when(jnp.logical_and(b < nu_ref[0], _new_expert(be_ref)))
    def _():
        wdb_ref[...] = wd_ref[0, 0].astype(BF16)

    @pl.when(b < nu_ref[0])
    def _():
        y = jnp.dot(h_ref[...], wdb_ref[...], preferred_element_type=F32)
        o_ref[...] = y * gt_ref[...]

    @pl.when(b >= nu_ref[0])
    def _():
        o_ref[...] = jnp.zeros(o_ref.shape, o_ref.dtype)


def _moe_experts(m, slot_tok, blk_expert, n_used, slot_gate, w_gate, w_up, w_down, layer):
    d = m.shape[1]
    n_slot = slot_tok.shape[0]
    de = w_gate.shape[3]
    n_blk = n_slot // MOE_BM
    hid = pl.pallas_call(
        _moe_up_kernel,
        grid_spec=pltpu.PrefetchScalarGridSpec(
            num_scalar_prefetch=3, grid=(n_blk,),
            in_specs=[pl.BlockSpec(memory_space=pl.ANY),
                      pl.BlockSpec((1, 1, d, de), lambda b, be, nu, tk: (layer, be[b], 0, 0)),
                      pl.BlockSpec((1, 1, d, de), lambda b, be, nu, tk: (layer, be[b], 0, 0))],
            out_specs=pl.BlockSpec((MOE_BM, de), lambda b, be, nu, tk: (b, 0)),
            scratch_shapes=[pltpu.VMEM((d, de), BF16), pltpu.VMEM((d, de), BF16),
                            pltpu.VMEM((2, MOE_BM, d), F32), pltpu.SemaphoreType.DMA((2,))]),
        out_shape=jax.ShapeDtypeStruct((n_slot, de), BF16),
        compiler_params=_cp("arbitrary"),
        name="moe_up",
    )(blk_expert, n_used, slot_tok, m, w_gate, w_up)
    return pl.pallas_call(
        _moe_down_kernel,
        grid_spec=pltpu.PrefetchScalarGridSpec(
            num_scalar_prefetch=2, grid=(n_blk,),
            in_specs=[pl.BlockSpec((MOE_BM, de), lambda b, be, nu: (b, 0)),
                      pl.BlockSpec((1, 1, de, d), lambda b, be, nu: (layer, be[b], 0, 0)),
                      pl.BlockSpec((MOE_BM, 1), lambda b, be, nu: (b, 0))],
            out_specs=pl.BlockSpec((MOE_BM, d), lambda b, be, nu: (b, 0)),
            scratch_shapes=[pltpu.VMEM((de, d), BF16)]),
        out_shape=jax.ShapeDtypeStruct((n_slot, d), F32),
        compiler_params=_cp("arbitrary"),
        name="moe_down",
    )(blk_expert, n_used, hid, w_down, slot_gate.reshape(n_slot, 1))


def _route(logits):
    n_exp = N_GROUPS * EXPERTS_PER_GROUP
    t = logits.shape[0]
    g_logits = logits[:, :N_GROUPS]
    g_prob = jax.nn.softmax(g_logits, axis=-1)
    g_sel = jnp.argmax(g_logits, axis=-1).astype(jnp.int32)
    g_w = jnp.take_along_axis(g_prob, g_sel[:, None], axis=1)[:, 0]
    e_logits = logits[:, N_GROUPS:N_GROUPS + n_exp].reshape(t, N_GROUPS, EXPERTS_PER_GROUP)
    e_logits = jnp.take_along_axis(e_logits, g_sel[:, None, None], axis=1)[:, 0]
    top_v, top_i = lax.top_k(e_logits, TOP_K)
    weights = g_w[:, None] * jax.nn.softmax(top_v, axis=-1)
    expert = g_sel[:, None] * EXPERTS_PER_GROUP + top_i.astype(jnp.int32)
    return expert, weights


def _dispatch(expert, weights):
    n_exp = N_GROUPS * EXPERTS_PER_GROUP
    t = expert.shape[0]
    n_as = t * TOP_K
    eid = expert.reshape(-1)
    gate = weights.reshape(-1)
    tok = jnp.repeat(jnp.arange(t, dtype=jnp.int32), TOP_K)
    order = jnp.argsort(eid).astype(jnp.int32)
    eid_s = eid[order]
    counts = jnp.zeros((n_exp,), jnp.int32).at[eid].add(1)
    padded = (counts + MOE_BM - 1) // MOE_BM * MOE_BM
    start = jnp.cumsum(counts) - counts
    pend = jnp.cumsum(padded)
    pstart = pend - padded
    dest = pstart[eid_s] + jnp.arange(n_as, dtype=jnp.int32) - start[eid_s]
    n_blk = -(-n_as // MOE_BM) + n_exp
    n_slot = n_blk * MOE_BM
    slot_src = jnp.full((n_slot,), n_as, jnp.int32).at[dest].set(order)
    slot_tok = jnp.concatenate([tok, jnp.zeros((1,), jnp.int32)])[slot_src]
    slot_gate = jnp.concatenate([gate, jnp.zeros((1,), gate.dtype)])[slot_src]
    blk_expert = jnp.searchsorted(pend, jnp.arange(n_blk, dtype=jnp.int32) * MOE_BM, side='right')
    blk_expert = jnp.minimum(blk_expert, n_exp - 1).astype(jnp.int32)
    n_used = (pend[-1] // MOE_BM).astype(jnp.int32).reshape(1)
    slot_of = jnp.zeros((n_as,), jnp.int32).at[order].set(dest).reshape(t, TOP_K)
    return slot_tok, slot_gate, blk_expert, n_used, slot_of.T.reshape(-1)


def _moe(m, logits, w_gate, w_up, w_down, layer):
    expert, weights = _route(logits)
    slot_tok, slot_gate, blk_expert, n_used, slot_idx = _dispatch(expert, weights)
    ys = _moe_experts(m, slot_tok, blk_expert, n_used, slot_gate, w_gate, w_up, w_down, layer)
    return ys, slot_idx


def _diff_lambda(lam_p, layer_idx):
    lam_init = 0.8 - 0.6 * math.exp(-0.3 * layer_idx)
    lf = lam_p.astype(F32)
    lam = jnp.exp(jnp.sum(lf[0] * lf[1])) - jnp.exp(jnp.sum(lf[2] * lf[3])) + lam_init
    return jnp.stack([lam, jnp.asarray(1.0 - lam_init, F32)]).astype(F32)


def kernel(x, c, ctx, c_ctx, w_mod, b_mod, norm1_g, norm2_g, att_w_in, att_w_out, att_lambda, att_subln_g,
           att_rpb, rec_w_in, rec_w_out, rec_lb_logits, rec_gnorm_g, moe_w_group, moe_b_group, moe_w_router,
           moe_b_router, moe_w_gate, moe_w_up, moe_w_down, final_norm_g):
    b_, n_lat, d = x.shape
    n_ctx = ctx.shape[1]
    assert b_ == 1 and n_ctx == ROW_TILE and n_lat % ROW_TILE == 0 and d % LANES == 0
    depth = w_mod.shape[0]
    daw, naw = DA_HEADS * LANES, NA_HEADS * LANES
    n_exp = N_GROUPS * EXPERTS_PER_GROUP

    lbp = jax.nn.softmax(rec_lb_logits.astype(F32), axis=0)
    lbs = jnp.cumsum(lbp, axis=0) - lbp[0:1]

    cvec = jnp.zeros((8, d), F32).at[0].set(c[0]).at[1].set(c_ctx)
    mods = _modulation(cvec, w_mod, b_mod)
    cos, sin = _rope_tables(n_lat, n_ctx)

    h = jnp.concatenate([x[0], ctx[0]], axis=0)
    a = _ln_mod(h, norm1_g[0], mods[0], 0, 1, n_lat)
    out = None
    for l in range(depth):
        j = l // 2
        mod = mods[l]
        if l % 2 == 0:
            p = _matmul(a, att_w_in, j, F32)
            qkv = _attn_prep(p, cos, sin, daw, naw)
            lam2 = _diff_lambda(att_lambda[j], l)
            da_lat = _diff_attn(qkv, lam2, att_subln_g[j], 0, n_lat, 0, n_lat + n_ctx, DA_HEADS)
            da_ctx = _diff_attn(qkv, lam2, att_subln_g[j], n_lat, n_ctx, n_lat, n_ctx, DA_HEADS)
            bias = _na_bias_tables(att_rpb[j], n_lat // GRID_W)
            na = _na_attn(qkv, bias, n_lat, n_ctx, DA_HEADS, NA_HEADS)
            cat = jnp.concatenate([jnp.concatenate([da_lat, da_ctx], axis=0), na], axis=1)
            y = _matmul(cat, att_w_out, j, F32)
        else:
            p = _matmul(a, rec_w_in, j, F32)
            o_f = _hgrn_dir(p, lbs[l, 0], False, n_lat, HG_HEADS)
            o = _hgrn_dir(p, lbs[l, 1], True, n_lat, HG_HEADS, o_fwd=o_f, gnorm_g=rec_gnorm_g[j])
            y = _matmul(o, rec_w_out, j, F32)
        w_route = jnp.zeros((d, LANES), F32).at[:, :N_GROUPS].set(moe_w_group[l])
        w_route = w_route.at[:, N_GROUPS:N_GROUPS + n_exp].set(moe_w_router[l])
        b_route = jnp.zeros((1, LANES), F32).at[0, :N_GROUPS].set(moe_b_group[l])
        b_route = b_route.at[0, N_GROUPS:N_GROUPS + n_exp].set(moe_b_router[l])
        h, m, logits = _res_ln_mod_route(h, y, norm2_g[l], mod, 2, 3, 4, w_route, b_route, n_lat)
        ys, slot_idx = _moe(m, logits, moe_w_gate, moe_w_up, moe_w_down, l)
        if l + 1 < depth:
            h, a = _res_ln_mod(h, ys, slot_idx, norm1_g[l + 1], mod, 5, mods[l + 1], 0, 1, n_lat)
        else:
            out = _res_final(h, ys, slot_idx, final_norm_g, mod, 5, n_lat)
    return out[None]
```

```python
import functools
import math

import numpy as np
import jax
import jax.numpy as jnp
from jax import lax
from jax.experimental import pallas as pl
from jax.experimental.pallas import tpu as pltpu

F32 = jnp.float32
BF16 = jnp.bfloat16

GRID_W = 64
EPS = 1e-6
DA_HEADS = 8
DA_QK_DIM = 64
NA_HEADS = 8
NA_WIN_ROWS = 8
NA_WIN_COLS = 16
ROPE_THETA = 10000.0
HG_HEADS = 16
N_GROUPS = 4
EXPERTS_PER_GROUP = 8
TOP_K = 2

LANES = 128
SUBLANES = 8
ROW_TILE = 256
HG_CHUNK = 64
HG_SUB = 32
HG_HPS = 2
NA_QROWS = 4
NA_KROWS = 12
MOE_BM = 256
NEG = -1e30
VMEM_LIMIT = 56 << 20


def _cp(*sem):
    return pltpu.CompilerParams(dimension_semantics=sem, vmem_limit_bytes=VMEM_LIMIT)


def _pick(n, cands):
    for c in cands:
        if n % c == 0:
            return c
    raise ValueError(f"no tile for {n} in {cands}")


def _silu(x):
    return x * jax.nn.sigmoid(x)


def _rms(x, g):
    return x * lax.rsqrt(jnp.mean(x * x, axis=-1, keepdims=True) + EPS) * g


def _mod_kernel(c_ref, w_ref, b_ref, o_ref):
    s = _silu(c_ref[...]).astype(BF16)
    o_ref[0] = jnp.dot(s, w_ref[0].astype(BF16), preferred_element_type=F32) + b_ref[0]


def _modulation(cvec, w_mod, b_mod):
    depth, d, n = w_mod.shape
    tn = _pick(n, (1024, 512, 256, 128))
    return pl.pallas_call(
        _mod_kernel,
        grid=(depth, n // tn),
        in_specs=[pl.BlockSpec((8, d), lambda l, j: (0, 0)),
                  pl.BlockSpec((1, d, tn), lambda l, j: (l, 0, j)),
                  pl.BlockSpec((1, 1, tn), lambda l, j: (l, 0, j))],
        out_specs=pl.BlockSpec((1, 8, tn), lambda l, j: (l, 0, j)),
        out_shape=jax.ShapeDtypeStruct((depth, 8, n), F32),
        compiler_params=_cp("arbitrary", "arbitrary"),
        name="modulation",
    )(cvec, w_mod, b_mod.reshape(depth, 1, n))


def _mod_row(ref, n_lat_tiles):
    r = (pl.program_id(0) >= n_lat_tiles).astype(jnp.int32)
    return ref[pl.ds(r, 1), :]


def _ln_mod_kernel(h_ref, g_ref, sh_ref, sc_ref, a_ref, *, n_lat_tiles):
    y = _rms(h_ref[...], g_ref[...])
    a = y * (1.0 + _mod_row(sc_ref, n_lat_tiles)) + _mod_row(sh_ref, n_lat_tiles)
    a_ref[...] = a.astype(a_ref.dtype)


def _row_gather(row_of, src_hbm, dst, sem, n_rows, start):
    def body(r, carry):
        cp = pltpu.make_async_copy(src_hbm.at[pl.ds(row_of(r), 1)], dst.at[pl.ds(r, 1)], sem)
        if start:
            cp.start()
        else:
            cp.wait()
        return carry

    lax.fori_loop(0, n_rows, body, 0, unroll=8)


def _combined_expert_rows(idx_ref, wt_ref, ys_hbm, ybuf, sem, t):
    i = pl.program_id(0)

    def gather(tile, start):
        slot = tile % 2
        for k in range(TOP_K):
            base = k * t + tile * ROW_TILE
            _row_gather(lambda r: idx_ref[base + r], ys_hbm, ybuf.at[slot, k], sem.at[slot], ROW_TILE, start)

    @pl.when(i == 0)
    def _():
        gather(i, True)

    @pl.when(i + 1 < pl.num_programs(0))
    def _():
        gather(i + 1, True)

    gather(i, False)
    y = wt_ref[0] * ybuf[i % 2, 0]
    for k in range(1, TOP_K):
        y = y + wt_ref[k] * ybuf[i % 2, k]
    return y


def _res_ln_mod_kernel(idx_ref, h_ref, wt_ref, ys_hbm, gt_ref, g_ref, sh_ref, sc_ref, ho_ref, a_ref, ybuf, sem,
                       *, n_lat_tiles, t):
    y = _combined_expert_rows(idx_ref, wt_ref, ys_hbm, ybuf, sem, t)
    h = h_ref[...] + _mod_row(gt_ref, n_lat_tiles) * y
    ho_ref[...] = h
    a = _rms(h, g_ref[...]) * (1.0 + _mod_row(sc_ref, n_lat_tiles)) + _mod_row(sh_ref, n_lat_tiles)
    a_ref[...] = a.astype(a_ref.dtype)


def _res_ln_mod_route_kernel(h_ref, y_ref, gt_ref, g_ref, sh_ref, sc_ref, wr_ref, br_ref,
                             ho_ref, a_ref, lg_ref, *, n_lat_tiles):
    h = h_ref[...] + _mod_row(gt_ref, n_lat_tiles) * y_ref[...]
    ho_ref[...] = h
    a = _rms(h, g_ref[...]) * (1.0 + _mod_row(sc_ref, n_lat_tiles)) + _mod_row(sh_ref, n_lat_tiles)
    a_ref[...] = a.astype(a_ref.dtype)
    lg_ref[...] = jnp.dot(a, wr_ref[...], preferred_element_type=F32,
                          precision=lax.Precision.HIGHEST) + br_ref[...]


def _res_final_kernel(idx_ref, h_ref, wt_ref, ys_hbm, gt_ref, g_ref, o_ref, ybuf, sem, *, t):
    y = _combined_expert_rows(idx_ref, wt_ref, ys_hbm, ybuf, sem, t)
    h = h_ref[...] + gt_ref[pl.ds(0, 1), :] * y
    o_ref[...] = _rms(h, g_ref[...])


def _row_spec(d):
    return pl.BlockSpec((ROW_TILE, d), lambda i, *_: (i, 0))


def _vec_spec(d):
    return pl.BlockSpec((1, d), lambda i, *_: (0, 0))


def _mod_spec(d, k):
    return pl.BlockSpec((8, d), lambda i, *_: (0, k))


def _combine_scratch(d):
    return [pltpu.VMEM((2, TOP_K, ROW_TILE, d), F32), pltpu.SemaphoreType.DMA((2,))]


def _wt_spec():
    return pl.BlockSpec((TOP_K, ROW_TILE, 1), lambda i, *_: (0, i, 0))


def _ln_mod(h, g, mod, k_shift, k_scale, n_lat):
    t, d = h.shape
    return pl.pallas_call(
        functools.partial(_ln_mod_kernel, n_lat_tiles=n_lat // ROW_TILE),
        grid=(t // ROW_TILE,),
        in_specs=[_row_spec(d), _vec_spec(d), _mod_spec(d, k_shift), _mod_spec(d, k_scale)],
        out_specs=_row_spec(d),
        out_shape=jax.ShapeDtypeStruct((t, d), BF16),
        compiler_params=_cp("arbitrary"),
        name="ln_mod",
    )(h, g.reshape(1, d), mod, mod)


def _res_ln_mod(h, ys, slot_idx, wt, g, mod_gate, k_gate, mod_next, k_shift, k_scale, n_lat):
    t, d = h.shape
    return pl.pallas_call(
        functools.partial(_res_ln_mod_kernel, n_lat_tiles=n_lat // ROW_TILE, t=t),
        grid_spec=pltpu.PrefetchScalarGridSpec(
            num_scalar_prefetch=1, grid=(t // ROW_TILE,),
            in_specs=[_row_spec(d), _wt_spec(), pl.BlockSpec(memory_space=pl.ANY), _mod_spec(d, k_gate),
                      _vec_spec(d), _mod_spec(d, k_shift), _mod_spec(d, k_scale)],
            out_specs=[_row_spec(d), _row_spec(d)],
            scratch_shapes=_combine_scratch(d)),
        out_shape=[jax.ShapeDtypeStruct((t, d), F32), jax.ShapeDtypeStruct((t, d), BF16)],
        compiler_params=_cp("arbitrary"),
        name="res_ln_mod",
    )(slot_idx, h, wt, ys, mod_gate, g.reshape(1, d), mod_next, mod_next)


def _res_ln_mod_route(h, y, g, mod, k_gate, k_shift, k_scale, w_route, b_route, n_lat):
    t, d = h.shape
    return pl.pallas_call(
        functools.partial(_res_ln_mod_route_kernel, n_lat_tiles=n_lat // ROW_TILE),
        grid=(t // ROW_TILE,),
        in_specs=[_row_spec(d), _row_spec(d), _mod_spec(d, k_gate), _vec_spec(d),
                  _mod_spec(d, k_shift), _mod_spec(d, k_scale),
                  pl.BlockSpec((d, LANES), lambda i: (0, 0)), _vec_spec(LANES)],
        out_specs=[_row_spec(d), _row_spec(d), _row_spec(LANES)],
        out_shape=[jax.ShapeDtypeStruct((t, d), F32), jax.ShapeDtypeStruct((t, d), F32),
                   jax.ShapeDtypeStruct((t, LANES), F32)],
        compiler_params=_cp("arbitrary"),
        name="res_ln_mod_route",
    )(h, y, mod, g.reshape(1, d), mod, mod, w_route, b_route)


def _res_final(h, ys, slot_idx, wt, g, mod, k_gate, n_lat):
    t, d = h.shape
    return pl.pallas_call(
        functools.partial(_res_final_kernel, t=t),
        grid_spec=pltpu.PrefetchScalarGridSpec(
            num_scalar_prefetch=1, grid=(n_lat // ROW_TILE,),
            in_specs=[_row_spec(d), _wt_spec(), pl.BlockSpec(memory_space=pl.ANY), _mod_spec(d, k_gate),
                      _vec_spec(d)],
            out_specs=_row_spec(d),
            scratch_shapes=_combine_scratch(d)),
        out_shape=jax.ShapeDtypeStruct((n_lat, d), F32),
        compiler_params=_cp("arbitrary"),
        name="res_final",
    )(slot_idx, h, wt, ys, mod, g.reshape(1, d))


def _matmul_kernel(a_ref, w_ref, o_ref, wb_ref):
    @pl.when(pl.program_id(1) == 0)
    def _():
        wb_ref[...] = w_ref[0].astype(BF16)

    o_ref[...] = jnp.dot(a_ref[...], wb_ref[...], preferred_element_type=F32).astype(o_ref.dtype)


def _matmul(a, w_stack, layer, out_dtype):
    m, k = a.shape
    n = w_stack.shape[2]
    tm = _pick(m, (768, 640, 512, 256))
    tn = _pick(n, (1024, 512, 256, 128))
    return pl.pallas_call(
        _matmul_kernel,
        grid=(n // tn, m // tm),
        in_specs=[pl.BlockSpec((tm, k), lambda j, i: (i, 0)),
                  pl.BlockSpec((1, k, tn), lambda j, i: (layer, 0, j))],
        out_specs=pl.BlockSpec((tm, tn), lambda j, i: (i, j)),
        out_shape=jax.ShapeDtypeStruct((m, n), out_dtype),
        scratch_shapes=[pltpu.VMEM((k, tn), BF16)],
        compiler_params=_cp("arbitrary", "arbitrary"),
        name="matmul",
    )(a, w_stack)


def _attn_prep_kernel(p_ref, cos_ref, sin_ref, o_ref, *, daw, naw, sa, sn):
    cos = cos_ref[...]
    sin = sin_ref[...]
    lane = lax.broadcasted_iota(jnp.int32, cos.shape, 1)
    first_half = (lane % (DA_QK_DIM // 2)) < (DA_QK_DIM // 4)
    quarter = DA_QK_DIM // 4
    for j in range((3 * daw + 3 * naw) // LANES):
        c0 = j * LANES
        x = p_ref[:, c0:c0 + LANES]
        if c0 < 2 * daw:
            partner = jnp.where(first_half, pltpu.roll(x, LANES - quarter, 1), pltpu.roll(x, quarter, 1))
            x = x * cos + partner * sin
            if c0 < daw:
                x = x * sa
        elif 3 * daw <= c0 < 3 * daw + naw:
            x = x * sn
        o_ref[:, c0:c0 + LANES] = x.astype(o_ref.dtype)


def _rope_tables(n_lat, n_ctx):
    t = jnp.arange(n_lat, dtype=jnp.int32)
    rows = (t // GRID_W).astype(F32)
    cols = (t % GRID_W).astype(F32)
    n_freq = DA_QK_DIM // 4
    inv_freq = ROPE_THETA ** (-jnp.arange(n_freq, dtype=F32) / n_freq)
    ang_r = rows[:, None] * inv_freq
    ang_c = cols[:, None] * inv_freq
    ang = jnp.concatenate([ang_r, ang_r, ang_c, ang_c], axis=1)
    sign = jnp.concatenate([-jnp.ones((n_freq,), F32), jnp.ones((n_freq,), F32)] * 2)
    cos = jnp.cos(ang)
    sin = jnp.sin(ang) * sign
    reps = LANES // DA_QK_DIM
    cos = jnp.concatenate([jnp.tile(cos, (1, reps)), jnp.ones((n_ctx, LANES), F32)], axis=0)
    sin = jnp.concatenate([jnp.tile(sin, (1, reps)), jnp.zeros((n_ctx, LANES), F32)], axis=0)
    return cos, sin


def _attn_prep(p, cos, sin, daw, naw):
    t, n = p.shape
    kern = functools.partial(_attn_prep_kernel, daw=daw, naw=naw, sa=DA_QK_DIM ** -0.5 * math.log2(math.e),
                             sn=LANES ** -0.5)
    return pl.pallas_call(
        kern,
        grid=(t // ROW_TILE,),
        in_specs=[_row_spec(n), _row_spec(LANES), _row_spec(LANES)],
        out_specs=_row_spec(n),
        out_shape=jax.ShapeDtypeStruct((t, n), BF16),
        compiler_params=_cp("arbitrary"),
        name="attn_prep",
    )(p, cos, sin)


def _diff_attn_kernel(lam_ref, q_ref, k_ref, v_ref, g_ref, o_ref, qm_ref, m_ref, acc_ref, *, nk, rg):
    kv = pl.program_id(2)
    tq = q_ref.shape[0]

    @pl.when(kv == 0)
    def _():
        q = q_ref[...]
        lane = lax.broadcasted_iota(jnp.int32, q.shape, 1)
        zero = jnp.zeros_like(q)
        qm_ref[0] = jnp.where(lane < DA_QK_DIM, q, zero)
        qm_ref[1] = jnp.where(lane >= DA_QK_DIM, q, zero)
        m_ref[...] = jnp.full(m_ref.shape, NEG, F32)
        acc_ref[...] = jnp.zeros(acc_ref.shape, F32)

    k = k_ref[...]
    v = v_ref[...]
    vext = jnp.concatenate([v, jnp.ones(v.shape, v.dtype)], axis=1)
    for r0 in range(0, tq, rg):
        for mi in range(2):
            s = lax.dot_general(qm_ref[mi, r0:r0 + rg], k, (((1,), (1,)), ((), ())), preferred_element_type=F32)
            m_prev = m_ref[mi, r0:r0 + rg]
            m_new = jnp.maximum(m_prev, jnp.max(s, axis=1, keepdims=True))
            p = jnp.exp2(s - m_new)
            acc_ref[mi, r0:r0 + rg] = (jnp.exp2(m_prev - m_new) * acc_ref[mi, r0:r0 + rg]
                                       + jnp.dot(p.astype(BF16), vext, preferred_element_type=F32))
            m_ref[mi, r0:r0 + rg] = m_new

    @pl.when(kv == nk - 1)
    def _():
        a0 = acc_ref[0]
        a1 = acc_ref[1]
        o = a0[:, :LANES] / a0[:, LANES:LANES + 1] - lam_ref[0] * (a1[:, :LANES] / a1[:, LANES:LANES + 1])
        o_ref[...] = (_rms(o, g_ref[...]) * lam_ref[1]).astype(o_ref.dtype)


DA_TQ, DA_TK, DA_RG = 1024, 2816, 256


def _diff_attn(qkv, lam2, subln_g, q_row0, n_q, k_row0, n_k, heads):
    tq = _pick(n_q, (DA_TQ, 1024, 512, 256))
    tk = _pick(n_k, (DA_TK, 1408, 1280, 1024, 768, 512, 256))
    assert q_row0 % tq == 0 and k_row0 % tk == 0
    nq, nk = n_q // tq, n_k // tk
    qb, kb = q_row0 // tq, k_row0 // tk
    return pl.pallas_call(
        functools.partial(_diff_attn_kernel, nk=nk, rg=min(DA_RG, tq)),
        grid=(heads, nq, nk),
        in_specs=[pl.BlockSpec(memory_space=pltpu.SMEM),
                  pl.BlockSpec((tq, LANES), lambda h, i, j: (qb + i, h)),
                  pl.BlockSpec((tk, LANES), lambda h, i, j: (kb + j, heads + h)),
                  pl.BlockSpec((tk, LANES), lambda h, i, j: (kb + j, 2 * heads + h)),
                  pl.BlockSpec((1, LANES), lambda h, i, j: (0, 0))],
        out_specs=pl.BlockSpec((tq, LANES), lambda h, i, j: (i, h)),
        out_shape=jax.ShapeDtypeStruct((n_q, heads * LANES), BF16),
        scratch_shapes=[pltpu.VMEM((2, tq, LANES), BF16), pltpu.VMEM((2, tq, 1), F32),
                        pltpu.VMEM((2, tq, 2 * LANES), F32)],
        compiler_params=_cp("arbitrary", "arbitrary", "arbitrary"),
        name="diff_attn",
    )(lam2, qkv, qkv, qkv, subln_g.reshape(1, LANES))


def _na_bias_tables(rpb, rows):
    wr, wc = NA_WIN_ROWS, NA_WIN_COLS
    heads = rpb.shape[0]
    c = np.arange(GRID_W)[:, None]
    kc = np.arange(GRID_W)[None, :]
    cs = np.clip(c - wc // 2, 0, GRID_W - wc)
    cvalid = (kc >= cs) & (kc < cs + wc)
    onehot = ((kc - c + wc - 1)[None] == np.arange(2 * wc - 1)[:, None, None]) & cvalid[None]
    sel = jnp.asarray(onehot.reshape(2 * wc - 1, GRID_W * GRID_W), F32)
    toe = jnp.dot(rpb.astype(F32).reshape(heads * (2 * wr - 1), 2 * wc - 1), sel,
                  precision=lax.Precision.HIGHEST)
    toe = toe.reshape(heads, 2 * wr - 1, GRID_W, GRID_W) + jnp.asarray(np.where(cvalid, 0.0, NEG), F32)
    toe = jnp.concatenate([toe, jnp.full((heads, 1, GRID_W, GRID_W), NEG, F32)], axis=1)
    r0s = np.array([0, NA_QROWS, rows - NA_QROWS])
    ks = np.clip(r0s - wr // 2, 0, rows - NA_KROWS)
    r = r0s[:, None, None] + np.arange(NA_QROWS)[None, :, None]
    kr = ks[:, None, None] + np.arange(NA_KROWS)[None, None, :]
    rs = np.clip(r - wr // 2, 0, rows - wr)
    idx = np.where((kr >= rs) & (kr < rs + wr), kr - r + wr - 1, 2 * wr - 1)
    blocks = toe[:, idx]
    return blocks.transpose(0, 1, 2, 4, 3, 5).reshape(heads, 3, NA_QROWS * GRID_W, NA_KROWS * GRID_W)


def _na_kernel(q_ref, k_ref, v_ref, b_ref, o_ref, *, n_lat, n_ctx, rows):
    rb = pl.program_id(1)
    n_rb = rows // NA_QROWS
    q = q_ref[...]
    kc = k_ref[n_lat:n_lat + n_ctx, :]
    vc = v_ref[n_lat:n_lat + n_ctx, :]
    nt = (((1,), (1,)), ((), ()))
    s_ctx = lax.dot_general(q, kc, nt, preferred_element_type=F32)

    @pl.when(rb < n_rb)
    def _():
        ks = jnp.clip(rb * NA_QROWS - NA_WIN_ROWS // 2, 0, rows - NA_KROWS)
        start = pl.multiple_of(ks * GRID_W, GRID_W)
        kw = k_ref[pl.ds(start, NA_KROWS * GRID_W), :]
        vw = v_ref[pl.ds(start, NA_KROWS * GRID_W), :]
        s_loc = lax.dot_general(q, kw, nt, preferred_element_type=F32) + b_ref[0, 0]
        m = jnp.maximum(jnp.max(s_loc, axis=1, keepdims=True), jnp.max(s_ctx, axis=1, keepdims=True))
        p_loc = jnp.exp(s_loc - m)
        p_ctx = jnp.exp(s_ctx - m)
        l = jnp.sum(p_loc, axis=1, keepdims=True) + jnp.sum(p_ctx, axis=1, keepdims=True)
        o = (jnp.dot(p_ctx.astype(BF16), vc, preferred_element_type=F32)
             + jnp.dot(p_loc.astype(BF16), vw, preferred_element_type=F32))
        o_ref[...] = (o / l).astype(o_ref.dtype)

    @pl.when(rb == n_rb)
    def _():
        m = jnp.max(s_ctx, axis=1, keepdims=True)
        p = jnp.exp(s_ctx - m)
        l = jnp.sum(p, axis=1, keepdims=True)
        o = jnp.dot(p.astype(BF16), vc, preferred_element_type=F32)
        o_ref[...] = (o / l).astype(o_ref.dtype)


def _na_attn(qkv, bias, n_lat, n_ctx, da_heads, heads):
    t = qkv.shape[0]
    rows = n_lat // GRID_W
    n_rb = rows // NA_QROWS
    tq = NA_QROWS * GRID_W
    assert tq == ROW_TILE and n_ctx == ROW_TILE
    q0, k0, v0 = 3 * da_heads, 3 * da_heads + heads, 3 * da_heads + 2 * heads

    def bias_map(h, rb):
        return (h, jnp.where(rb == 0, 0, jnp.where(rb == n_rb - 1, 2, 1)), 0, 0)

    return pl.pallas_call(
        functools.partial(_na_kernel, n_lat=n_lat, n_ctx=n_ctx, rows=rows),
        grid=(heads, n_rb + 1),
        in_specs=[pl.BlockSpec((tq, LANES), lambda h, rb: (rb, q0 + h)),
                  pl.BlockSpec((t, LANES), lambda h, rb: (0, k0 + h)),
                  pl.BlockSpec((t, LANES), lambda h, rb: (0, v0 + h)),
                  pl.BlockSpec((1, 1, tq, NA_KROWS * GRID_W), bias_map)],
        out_specs=pl.BlockSpec((tq, LANES), lambda h, rb: (rb, h)),
        out_shape=jax.ShapeDtypeStruct((t, heads * LANES), BF16),
        compiler_params=_cp("arbitrary", "arbitrary"),
        name="na_attn",
    )(qkv, qkv, qkv, bias)


def _hgrn_prep(q, v, z, lb, rev):
    c = HG_CHUNK
    f = lb + (1.0 - lb) * jax.nn.sigmoid(z)
    kk = 1.0 - f
    g = jnp.log2(f)
    ri = lax.broadcasted_iota(jnp.int32, (c, c), 0)
    ci = lax.broadcasted_iota(jnp.int32, (c, c), 1)
    tri = ((ri <= ci) if rev else (ri >= ci)).astype(BF16)
    g_hi = g.astype(BF16)
    r1 = g - g_hi.astype(F32)
    g_mid = r1.astype(BF16)
    g_lo = (r1 - g_mid.astype(F32)).astype(BF16)
    c2 = jnp.dot(tri, jnp.concatenate([g_hi, g_mid], axis=1), preferred_element_type=F32)
    cum = c2[:, :LANES] + c2[:, LANES:] + jnp.dot(tri, g_lo, preferred_element_type=F32)
    total = jnp.sum(g, axis=0, keepdims=True)
    return dict(cum=cum, total=total, kk=kk, qs=_silu(q) * (LANES ** -0.5), v=v, vb=v.astype(BF16))


def _hgrn_intra(pp, rev):
    c, sub, nsub = HG_CHUNK, HG_SUB, HG_CHUNK // HG_SUB
    nt = (((1,), (1,)), ((), ()))
    cum, kk, qs, v, vb = pp["cum"], pp["kk"], pp["qs"], pp["v"], pp["vb"]
    ngrp = sub // SUBLANES
    ti = lax.broadcasted_iota(jnp.int32, (SUBLANES, LANES), 0)
    ckey = cum - jnp.log2(kk)
    o_blocks = [None] * nsub
    for i in range(nsub):
        lo = (nsub - 1 - i) * sub if rev else i * sub
        hi = lo + sub
        cum_i, q_i = cum[lo:hi], qs[lo:hi]
        og = [None] * ngrp
        if i > 0:
            brow = hi if rev else lo - 1
            bnd = cum[brow:brow + 1]
            elo, ehi = (hi, c) if rev else (0, lo)
            qt = (q_i * jnp.exp2(cum_i - bnd)).astype(BF16)
            kt = (kk[elo:ehi] * jnp.exp2(bnd - cum[elo:ehi])).astype(BF16)
            att = lax.dot_general(qt, kt, nt, preferred_element_type=F32)
            o_i = jnp.dot(att.astype(BF16), vb[elo:ehi], preferred_element_type=F32)
            og = [o_i[gq * SUBLANES:(gq + 1) * SUBLANES] for gq in range(ngrp)]
        zs, dst = [], []
        for s in range(sub):
            gs = s // SUBLANES
            for gq in (range(gs + 1) if rev else range(gs, ngrp)):
                r = slice(gq * SUBLANES, (gq + 1) * SUBLANES)
                dlt = cum_i[r] - ckey[lo + s:lo + s + 1]
                if gq == gs:
                    dlt = jnp.where((ti <= s - r.start) if rev else (ti >= s - r.start), dlt, NEG)
                zs.append(jnp.exp2(dlt) * q_i[r])
                dst.append((gq, s))
        a_all = jnp.sum(jnp.concatenate(zs, axis=0), axis=1, keepdims=True)
        for n, (gq, s) in enumerate(dst):
            term = a_all[n * SUBLANES:(n + 1) * SUBLANES] * v[lo + s:lo + s + 1]
            og[gq] = term if og[gq] is None else og[gq] + term
        o_blocks[(nsub - 1 - i) if rev else i] = jnp.concatenate(og, axis=0)
    return jnp.concatenate(o_blocks, axis=0)


def _hgrn_carry(pp, st):
    cum, total = pp["cum"], pp["total"]
    o_inter = lax.dot_general((pp["qs"] * jnp.exp2(cum)).astype(BF16), st.astype(BF16),
                              (((1,), (1,)), ((), ())), preferred_element_type=F32)
    kdec = (pp["kk"] * jnp.exp2(total - cum)).astype(BF16)
    st_new = st * jnp.exp2(total) + lax.dot_general(pp["vb"], kdec, (((0,), (0,)), ((), ())),
                                                    preferred_element_type=F32)
    return o_inter, st_new


def _hgrn_kernel(*refs, rev, final, n_chunks):
    if final:
        q_ref, v_ref, z_ref, lb_ref, of_ref, gate_ref, gn_ref, o_ref, st_ref = refs
    else:
        q_ref, v_ref, z_ref, lb_ref, o_ref, st_ref = refs

    @pl.when(pl.program_id(1) == 0)
    def _():
        st_ref[...] = jnp.zeros(st_ref.shape, F32)

    order = [(n_chunks - 1 - s) if rev else s for s in range(n_chunks)]
    tiles = [(slice(ch * HG_CHUNK, (ch + 1) * HG_CHUNK), hh, slice(hh * LANES, (hh + 1) * LANES))
             for ch in order for hh in range(HG_HPS)]
    preps = [_hgrn_prep(q_ref[rows, cols], v_ref[rows, cols], z_ref[rows, cols], lb_ref[:, cols], rev)
             for rows, _, cols in tiles]
    intras = [_hgrn_intra(pp, rev) for pp in preps]
    states = [st_ref[hh] for hh in range(HG_HPS)]
    for (rows, hh, cols), pp, o in zip(tiles, preps, intras):
        o_inter, states[hh] = _hgrn_carry(pp, states[hh])
        o = o + o_inter
        if final:
            o = _rms(o + of_ref[rows, cols], gn_ref[...]) * _silu(gate_ref[rows, cols])
        o_ref[rows, cols] = o.astype(o_ref.dtype)
    for hh in range(HG_HPS):
        st_ref[hh] = states[hh]


def _hgrn_dir(p, lb_dir, rev, n_lat, heads, o_fwd=None, gnorm_g=None):
    t = p.shape[0]
    d = heads * LANES
    n_blk = t // ROW_TILE
    last = n_blk - 1
    final = o_fwd is not None

    def tok(j):
        return jnp.where(j == 0, last, (last - j) if rev else (j - 1))

    assert heads % HG_HPS == 0
    hg = heads // HG_HPS
    w = HG_HPS * LANES
    zcol = 3 * hg if rev else 2 * hg
    in_specs = [pl.BlockSpec((ROW_TILE, w), lambda h, j: (tok(j), h)),
                pl.BlockSpec((ROW_TILE, w), lambda h, j: (tok(j), hg + h)),
                pl.BlockSpec((ROW_TILE, w), lambda h, j: (tok(j), zcol + h)),
                pl.BlockSpec((1, w), lambda h, j: (0, h))]
    args = [p, p, p, lb_dir.reshape(1, d)]
    if final:
        in_specs += [pl.BlockSpec((ROW_TILE, w), lambda h, j: (tok(j), h)),
                     pl.BlockSpec((ROW_TILE, w), lambda h, j: (tok(j), 4 * hg + h)),
                     pl.BlockSpec((1, LANES), lambda h, j: (0, 0))]
        args += [o_fwd, p, gnorm_g.reshape(1, LANES)]
    return pl.pallas_call(
        functools.partial(_hgrn_kernel, rev=rev, final=final, n_chunks=ROW_TILE // HG_CHUNK),
        grid=(hg, n_blk),
        in_specs=in_specs,
        out_specs=pl.BlockSpec((ROW_TILE, w), lambda h, j: (tok(j), h)),
        out_shape=jax.ShapeDtypeStruct((t, d), BF16 if final else F32),
        scratch_shapes=[pltpu.VMEM((HG_HPS, LANES, LANES), F32)],
        compiler_params=_cp("arbitrary", "arbitrary"),
        name="hgrn_bwd" if rev else "hgrn_fwd",
    )(*args)


def _new_expert(be_ref):
    b = pl.program_id(0)
    return jnp.logical_or(b == 0, be_ref[b] != be_ref[jnp.maximum(b - 1, 0)])


def _moe_up_kernel(be_ref, nu_ref, base_ref, nval_ref, tok_ref, x_hbm, wg_ref, wu_ref, o_ref,
                   wgb_ref, wub_ref, xbuf, sem):
    b = pl.program_id(0)

    def gather(blk, start):
        base = base_ref[blk]
        last = jnp.maximum(nval_ref[blk] - 1, 0)
        _row_gather(lambda r: tok_ref[base + jnp.minimum(r, last)], x_hbm, xbuf.at[blk % 2], sem.at[blk % 2],
                    MOE_BM, start)

    @pl.when(b == 0)
    def _():
        gather(b, True)

    @pl.when(b + 1 < nu_ref[0])
    def _():
        gather(b + 1, True)

    @pl.when(jnp.logical_and(b < nu_ref[0], _new_expert(be_ref)))
    def _():
        wgb_ref[...] = wg_ref[0, 0].astype(BF16)
        wub_ref[...] = wu_ref[0, 0].astype(BF16)

    @pl.when(b < nu_ref[0])
    def _():
        gather(b, False)
        x = xbuf[b % 2].astype(BF16)
        hg = jnp.dot(x, wgb_ref[...], preferred_element_type=F32)
        hu = jnp.dot(x, wub_ref[...], preferred_element_type=F32)
        o_ref[...] = (_silu(hg) * hu).astype(o_ref.dtype)

    @pl.when(b >= nu_ref[0])
    def _():
        o_ref[...] = jnp.zeros(o_ref.shape, o_ref.dtype)


def _moe_down_kernel(be_ref, nu_ref, h_ref, wd_ref, o_ref, wdb_ref):
    b = pl.program_id(0)

    @pl.when(jnp.logical_and(b < nu_ref[0], _new_expert(be_ref)))
    def _():
        wdb_ref[...] = wd_ref[0, 0].astype(BF16)

    @pl.when(b < nu_ref[0])
    def _():
        o_ref[...] = jnp.dot(h_ref[...], wdb_ref[...], preferred_element_type=F32)

    @pl.when(b >= nu_ref[0])
    def _():
        o_ref[...] = jnp.zeros(o_ref.shape, o_ref.dtype)


def _moe_experts(m, tok_sorted, blk_expert, blk_base, blk_nval, n_used, w_gate, w_up, w_down, layer):
    d = m.shape[1]
    n_blk = blk_expert.shape[0]
    n_slot = n_blk * MOE_BM
    de = w_gate.shape[3]
    hid = pl.pallas_call(
        _moe_up_kernel,
        grid_spec=pltpu.PrefetchScalarGridSpec(
            num_scalar_prefetch=5, grid=(n_blk,),
            in_specs=[pl.BlockSpec(memory_space=pl.ANY),
                      pl.BlockSpec((1, 1, d, de), lambda b, be, *_: (layer, be[b], 0, 0)),
                      pl.BlockSpec((1, 1, d, de), lambda b, be, *_: (layer, be[b], 0, 0))],
            out_specs=pl.BlockSpec((MOE_BM, de), lambda b, *_: (b, 0)),
            scratch_shapes=[pltpu.VMEM((d, de), BF16), pltpu.VMEM((d, de), BF16),
                            pltpu.VMEM((2, MOE_BM, d), F32), pltpu.SemaphoreType.DMA((2,))]),
        out_shape=jax.ShapeDtypeStruct((n_slot, de), BF16),
        compiler_params=_cp("arbitrary"),
        name="moe_up",
    )(blk_expert, n_used, blk_base, blk_nval, tok_sorted, m, w_gate, w_up)
    return pl.pallas_call(
        _moe_down_kernel,
        grid_spec=pltpu.PrefetchScalarGridSpec(
            num_scalar_prefetch=2, grid=(n_blk,),
            in_specs=[pl.BlockSpec((MOE_BM, de), lambda b, be, nu: (b, 0)),
                      pl.BlockSpec((1, 1, de, d), lambda b, be, nu: (layer, be[b], 0, 0))],
            out_specs=pl.BlockSpec((MOE_BM, d), lambda b, be, nu: (b, 0)),
            scratch_shapes=[pltpu.VMEM((de, d), BF16)]),
        out_shape=jax.ShapeDtypeStruct((n_slot, d), F32),
        compiler_params=_cp("arbitrary"),
        name="moe_down",
    )(blk_expert, n_used, hid, w_down)


def _route(logits):
    n_exp = N_GROUPS * EXPERTS_PER_GROUP
    t = logits.shape[0]
    g_logits = logits[:, :N_GROUPS]
    g_prob = jax.nn.softmax(g_logits, axis=-1)
    g_sel = jnp.argmax(g_logits, axis=-1).astype(jnp.int32)
    g_w = jnp.take_along_axis(g_prob, g_sel[:, None], axis=1)[:, 0]
    e_logits = logits[:, N_GROUPS:N_GROUPS + n_exp].reshape(t, N_GROUPS, EXPERTS_PER_GROUP)
    e_logits = jnp.take_along_axis(e_logits, g_sel[:, None, None], axis=1)[:, 0]
    top_v, top_i = lax.top_k(e_logits, TOP_K)
    weights = g_w[:, None] * jax.nn.softmax(top_v, axis=-1)
    expert = g_sel[:, None] * EXPERTS_PER_GROUP + top_i.astype(jnp.int32)
    return expert, weights


def _dispatch(expert, weights):
    n_exp = N_GROUPS * EXPERTS_PER_GROUP
    t = expert.shape[0]
    n_as = t * TOP_K
    eid = expert.reshape(-1)
    aidx = jnp.arange(n_as, dtype=jnp.int32)
    eid_s, order = lax.sort((eid, aidx), num_keys=1, is_stable=True)
    onehot = eid_s[:, None] == jnp.arange(n_exp, dtype=jnp.int32)[None, :]
    counts = jnp.sum(onehot, axis=0, dtype=jnp.int32)
    padded = (counts + MOE_BM - 1) // MOE_BM * MOE_BM
    start = jnp.cumsum(counts) - counts
    pend = jnp.cumsum(padded)
    pstart = pend - padded
    dest = aidx + jnp.sum(jnp.where(onehot, (pstart - start)[None, :], 0), axis=1)
    n_blk = -(-n_as // MOE_BM) + n_exp
    blk_expert = jnp.searchsorted(pend, jnp.arange(n_blk, dtype=jnp.int32) * MOE_BM, side='right')
    blk_expert = jnp.minimum(blk_expert, n_exp - 1).astype(jnp.int32)
    n_used = (pend[-1] // MOE_BM).astype(jnp.int32).reshape(1)
    blk_first = jnp.arange(n_blk, dtype=jnp.int32) * MOE_BM - pstart[blk_expert]
    blk_base = jnp.clip(start[blk_expert] + blk_first, 0, n_as - 1).astype(jnp.int32)
    blk_nval = jnp.clip(counts[blk_expert] - blk_first, 0, MOE_BM).astype(jnp.int32)
    blk_base = jnp.minimum(blk_base, n_as - jnp.maximum(blk_nval, 1))
    _, slot_of = lax.sort((order, dest), num_keys=1)
    slot_idx = slot_of.reshape(t, TOP_K).T.reshape(-1)
    return order // TOP_K, blk_expert, blk_base, blk_nval, n_used, slot_idx


def _moe(m, logits, w_gate, w_up, w_down, layer):
    expert, weights = _route(logits)
    tok_sorted, blk_expert, blk_base, blk_nval, n_used, slot_idx = _dispatch(expert, weights)
    ys = _moe_experts(m, tok_sorted, blk_expert, blk_base, blk_nval, n_used, w_gate, w_up, w_down, layer)
    return ys, slot_idx, weights.T[:, :, None]


def _diff_lambda(lam_p, layer_idx):
    lam_init = 0.8 - 0.6 * math.exp(-0.3 * layer_idx)
    lf = lam_p.astype(F32)
    lam = jnp.exp(jnp.sum(lf[0] * lf[1])) - jnp.exp(jnp.sum(lf[2] * lf[3])) + lam_init
    return jnp.stack([lam, jnp.asarray(1.0 - lam_init, F32)]).astype(F32)


def kernel(x, c, ctx, c_ctx, w_mod, b_mod, norm1_g, norm2_g, att_w_in, att_w_out, att_lambda, att_subln_g,
           att_rpb, rec_w_in, rec_w_out, rec_lb_logits, rec_gnorm_g, moe_w_group, moe_b_group, moe_w_router,
           moe_b_router, moe_w_gate, moe_w_up, moe_w_down, final_norm_g):
    b_, n_lat, d = x.shape
    n_ctx = ctx.shape[1]
    assert b_ == 1 and n_ctx == ROW_TILE and n_lat % ROW_TILE == 0 and d % LANES == 0
    depth = w_mod.shape[0]
    daw, naw = DA_HEADS * LANES, NA_HEADS * LANES
    n_exp = N_GROUPS * EXPERTS_PER_GROUP

    lbp = jax.nn.softmax(rec_lb_logits.astype(F32), axis=0)
    lbs = jnp.cumsum(lbp, axis=0) - lbp[0:1]

    cvec = jnp.zeros((8, d), F32).at[0].set(c[0]).at[1].set(c_ctx)
    mods = _modulation(cvec, w_mod, b_mod)
    cos, sin = _rope_tables(n_lat, n_ctx)

    h = jnp.concatenate([x[0], ctx[0]], axis=0)
    a = _ln_mod(h, norm1_g[0], mods[0], 0, 1, n_lat)
    out = None
    for l in range(depth):
        j = l // 2
        mod = mods[l]
        if l % 2 == 0:
            p = _matmul(a, att_w_in, j, F32)
            qkv = _attn_prep(p, cos, sin, daw, naw)
            lam2 = _diff_lambda(att_lambda[j], l)
            da_lat = _diff_attn(qkv, lam2, att_subln_g[j], 0, n_lat, 0, n_lat + n_ctx, DA_HEADS)
            da_ctx = _diff_attn(qkv, lam2, att_subln_g[j], n_lat, n_ctx, n_lat, n_ctx, DA_HEADS)
            bias = _na_bias_tables(att_rpb[j], n_lat // GRID_W)
            na = _na_attn(qkv, bias, n_lat, n_ctx, DA_HEADS, NA_HEADS)
            cat = jnp.concatenate([jnp.concatenate([da_lat, da_ctx], axis=0), na], axis=1)
            y = _matmul(cat, att_w_out, j, F32)
        else:
            p = _matmul(a, rec_w_in, j, F32)
            o_f = _hgrn_dir(p, lbs[l, 0], False, n_lat, HG_HEADS)
            o = _hgrn_dir(p, lbs[l, 1], True, n_lat, HG_HEADS, o_fwd=o_f, gnorm_g=rec_gnorm_g[j])
            y = _matmul(o, rec_w_out, j, F32)
        w_route = jnp.zeros((d, LANES), F32).at[:, :N_GROUPS].set(moe_w_group[l])
        w_route = w_route.at[:, N_GROUPS:N_GROUPS + n_exp].set(moe_w_router[l])
        b_route = jnp.zeros((1, LANES), F32).at[0, :N_GROUPS].set(moe_b_group[l])
        b_route = b_route.at[0, N_GROUPS:N_GROUPS + n_exp].set(moe_b_router[l])
        h, m, logits = _res_ln_mod_route(h, y, norm2_g[l], mod, 2, 3, 4, w_route, b_route, n_lat)
        ys, slot_idx, wt = _moe(m, logits, moe_w_gate, moe_w_up, moe_w_down, l)
        if l + 1 < depth:
            h, a = _res_ln_mod(h, ys, slot_idx, wt, norm1_g[l + 1], mod, 5, mods[l + 1], 0, 1, n_lat)
        else:
            out = _res_final(h, ys, slot_idx, wt, final_norm_g, mod, 5, n_lat)
    return out[None]
```

```python
import functools
import math

import numpy as np
import jax
import jax.numpy as jnp
from jax import lax
from jax.experimental import pallas as pl
from jax.experimental.pallas import tpu as pltpu

F32 = jnp.float32
BF16 = jnp.bfloat16

GRID_W = 64
EPS = 1e-6
DA_HEADS = 8
DA_QK_DIM = 64
NA_HEADS = 8
NA_WIN_ROWS = 8
NA_WIN_COLS = 16
ROPE_THETA = 10000.0
HG_HEADS = 16
N_GROUPS = 4
EXPERTS_PER_GROUP = 8
TOP_K = 2

LANES = 128
SUBLANES = 8
ROW_TILE = 256
HG_CHUNK = 64
HG_SUB = 32
HG_HPS = 2
NA_QROWS = 4
NA_KROWS = 12
MOE_BM = 256
NEG = -1e30
VMEM_LIMIT = 56 << 20


def _cp(*sem):
    return pltpu.CompilerParams(dimension_semantics=sem, vmem_limit_bytes=VMEM_LIMIT)


def _pick(n, cands):
    for c in cands:
        if n % c == 0:
            return c
    raise ValueError(f"no tile for {n} in {cands}")


def _silu(x):
    return x * jax.nn.sigmoid(x)


def _rms(x, g):
    return x * lax.rsqrt(jnp.mean(x * x, axis=-1, keepdims=True) + EPS) * g


def _mod_kernel(c_ref, w_ref, b_ref, o_ref):
    s = _silu(c_ref[...]).astype(BF16)
    o_ref[0] = jnp.dot(s, w_ref[0].astype(BF16), preferred_element_type=F32) + b_ref[0]


def _modulation(cvec, w_mod, b_mod):
    depth, d, n = w_mod.shape
    tn = _pick(n, (1024, 512, 256, 128))
    return pl.pallas_call(
        _mod_kernel,
        grid=(depth, n // tn),
        in_specs=[pl.BlockSpec((8, d), lambda l, j: (0, 0)),
                  pl.BlockSpec((1, d, tn), lambda l, j: (l, 0, j)),
                  pl.BlockSpec((1, 1, tn), lambda l, j: (l, 0, j))],
        out_specs=pl.BlockSpec((1, 8, tn), lambda l, j: (l, 0, j)),
        out_shape=jax.ShapeDtypeStruct((depth, 8, n), F32),
        compiler_params=_cp("arbitrary", "arbitrary"),
        name="modulation",
    )(cvec, w_mod, b_mod.reshape(depth, 1, n))


def _mod_row(ref, n_lat_tiles):
    r = (pl.program_id(0) >= n_lat_tiles).astype(jnp.int32)
    return ref[pl.ds(r, 1), :]


def _ln_mod_kernel(h_ref, g_ref, sh_ref, sc_ref, a_ref, *, n_lat_tiles):
    y = _rms(h_ref[...], g_ref[...])
    a = y * (1.0 + _mod_row(sc_ref, n_lat_tiles)) + _mod_row(sh_ref, n_lat_tiles)
    a_ref[...] = a.astype(a_ref.dtype)


def _row_gather(row_of, src_hbm, dst, sem, n_rows, start, span=1):
    def body(r, carry):
        cp = pltpu.make_async_copy(src_hbm.at[pl.ds(pl.multiple_of(row_of(r) * span, span), span)],
                                   dst.at[pl.ds(pl.multiple_of(r * span, span), span)], sem)
        if start:
            cp.start()
        else:
            cp.wait()
        return carry

    lax.fori_loop(0, n_rows, body, 0, unroll=8)


def _combined_expert_rows(idx_ref, wt_ref, ys_hbm, ybuf, sem, t):
    i = pl.program_id(0)

    def gather(tile, start):
        slot = tile % 2
        for k in range(TOP_K):
            base = k * t + tile * ROW_TILE
            _row_gather(lambda r: idx_ref[base + r], ys_hbm, ybuf.at[slot, k], sem.at[slot], ROW_TILE, start)

    @pl.when(i == 0)
    def _():
        gather(i, True)

    @pl.when(i + 1 < pl.num_programs(0))
    def _():
        gather(i + 1, True)

    gather(i, False)
    y = wt_ref[0] * ybuf[i % 2, 0]
    for k in range(1, TOP_K):
        y = y + wt_ref[k] * ybuf[i % 2, k]
    return y


def _res_ln_mod_kernel(idx_ref, h_ref, wt_ref, ys_hbm, gt_ref, g_ref, sh_ref, sc_ref, ho_ref, a_ref, ybuf, sem,
                       *, n_lat_tiles, t):
    y = _combined_expert_rows(idx_ref, wt_ref, ys_hbm, ybuf, sem, t)
    h = h_ref[...] + _mod_row(gt_ref, n_lat_tiles) * y
    ho_ref[...] = h
    a = _rms(h, g_ref[...]) * (1.0 + _mod_row(sc_ref, n_lat_tiles)) + _mod_row(sh_ref, n_lat_tiles)
    a_ref[...] = a.astype(a_ref.dtype)


def _res_ln_mod_route_kernel(h_ref, y_ref, gt_ref, g_ref, sh_ref, sc_ref, wr_ref, br_ref,
                             ho_ref, a_ref, lg_ref, *, n_lat_tiles):
    h = h_ref[...] + _mod_row(gt_ref, n_lat_tiles) * y_ref[...]
    ho_ref[...] = h
    a = _rms(h, g_ref[...]) * (1.0 + _mod_row(sc_ref, n_lat_tiles)) + _mod_row(sh_ref, n_lat_tiles)
    nch = a.shape[1] // LANES
    for j in range(nch):
        a_ref[pl.ds(j, ROW_TILE, stride=nch), :] = a[:, j * LANES:(j + 1) * LANES]
    lg_ref[...] = jnp.dot(a, wr_ref[...], preferred_element_type=F32,
                          precision=lax.Precision.HIGHEST) + br_ref[...]


def _res_final_kernel(idx_ref, h_ref, wt_ref, ys_hbm, gt_ref, g_ref, o_ref, ybuf, sem, *, t):
    y = _combined_expert_rows(idx_ref, wt_ref, ys_hbm, ybuf, sem, t)
    h = h_ref[...] + gt_ref[pl.ds(0, 1), :] * y
    o_ref[...] = _rms(h, g_ref[...])


def _row_spec(d):
    return pl.BlockSpec((ROW_TILE, d), lambda i, *_: (i, 0))


def _vec_spec(d):
    return pl.BlockSpec((1, d), lambda i, *_: (0, 0))


def _mod_spec(d, k):
    return pl.BlockSpec((8, d), lambda i, *_: (0, k))


def _combine_scratch(d):
    return [pltpu.VMEM((2, TOP_K, ROW_TILE, d), F32), pltpu.SemaphoreType.DMA((2,))]


def _wt_spec():
    return pl.BlockSpec((TOP_K, ROW_TILE, 1), lambda i, *_: (0, i, 0))


def _ln_mod(h, g, mod, k_shift, k_scale, n_lat):
    t, d = h.shape
    return pl.pallas_call(
        functools.partial(_ln_mod_kernel, n_lat_tiles=n_lat // ROW_TILE),
        grid=(t // ROW_TILE,),
        in_specs=[_row_spec(d), _vec_spec(d), _mod_spec(d, k_shift), _mod_spec(d, k_scale)],
        out_specs=_row_spec(d),
        out_shape=jax.ShapeDtypeStruct((t, d), BF16),
        compiler_params=_cp("arbitrary"),
        name="ln_mod",
    )(h, g.reshape(1, d), mod, mod)


def _res_ln_mod(h, ys, slot_idx, wt, g, mod_gate, k_gate, mod_next, k_shift, k_scale, n_lat):
    t, d = h.shape
    return pl.pallas_call(
        functools.partial(_res_ln_mod_kernel, n_lat_tiles=n_lat // ROW_TILE, t=t),
        grid_spec=pltpu.PrefetchScalarGridSpec(
            num_scalar_prefetch=1, grid=(t // ROW_TILE,),
            in_specs=[_row_spec(d), _wt_spec(), pl.BlockSpec(memory_space=pl.ANY), _mod_spec(d, k_gate),
                      _vec_spec(d), _mod_spec(d, k_shift), _mod_spec(d, k_scale)],
            out_specs=[_row_spec(d), _row_spec(d)],
            scratch_shapes=_combine_scratch(d)),
        out_shape=[jax.ShapeDtypeStruct((t, d), F32), jax.ShapeDtypeStruct((t, d), BF16)],
        compiler_params=_cp("arbitrary"),
        name="res_ln_mod",
    )(slot_idx, h, wt, ys, mod_gate, g.reshape(1, d), mod_next, mod_next)


def _res_ln_mod_route(h, y, g, mod, k_gate, k_shift, k_scale, w_route, b_route, n_lat):
    t, d = h.shape
    return pl.pallas_call(
        functools.partial(_res_ln_mod_route_kernel, n_lat_tiles=n_lat // ROW_TILE),
        grid=(t // ROW_TILE,),
        in_specs=[_row_spec(d), _row_spec(d), _mod_spec(d, k_gate), _vec_spec(d),
                  _mod_spec(d, k_shift), _mod_spec(d, k_scale),
                  pl.BlockSpec((d, LANES), lambda i: (0, 0)), _vec_spec(LANES)],
        out_specs=[_row_spec(d), pl.BlockSpec((ROW_TILE * (d // LANES), LANES), lambda i: (i, 0)),
                   _row_spec(LANES)],
        out_shape=[jax.ShapeDtypeStruct((t, d), F32), jax.ShapeDtypeStruct((t * (d // LANES), LANES), F32),
                   jax.ShapeDtypeStruct((t, LANES), F32)],
        compiler_params=_cp("arbitrary"),
        name="res_ln_mod_route",
    )(h, y, mod, g.reshape(1, d), mod, mod, w_route, b_route)


def _res_final(h, ys, slot_idx, wt, g, mod, k_gate, n_lat):
    t, d = h.shape
    return pl.pallas_call(
        functools.partial(_res_final_kernel, t=t),
        grid_spec=pltpu.PrefetchScalarGridSpec(
            num_scalar_prefetch=1, grid=(n_lat // ROW_TILE,),
            in_specs=[_row_spec(d), _wt_spec(), pl.BlockSpec(memory_space=pl.ANY), _mod_spec(d, k_gate),
                      _vec_spec(d)],
            out_specs=_row_spec(d),
            scratch_shapes=_combine_scratch(d)),
        out_shape=jax.ShapeDtypeStruct((n_lat, d), F32),
        compiler_params=_cp("arbitrary"),
        name="res_final",
    )(slot_idx, h, wt, ys, mod, g.reshape(1, d))


def _matmul_kernel(a_ref, w_ref, o_ref, wb_ref):
    @pl.when(pl.program_id(1) == 0)
    def _():
        wb_ref[...] = w_ref[0].astype(BF16)

    o_ref[...] = jnp.dot(a_ref[...], wb_ref[...], preferred_element_type=F32).astype(o_ref.dtype)


def _matmul(a, w_stack, layer, out_dtype):
    m, k = a.shape
    n = w_stack.shape[2]
    tm = _pick(m, (768, 640, 512, 256))
    tn = _pick(n, (1024, 512, 256, 128))
    return pl.pallas_call(
        _matmul_kernel,
        grid=(n // tn, m // tm),
        in_specs=[pl.BlockSpec((tm, k), lambda j, i: (i, 0)),
                  pl.BlockSpec((1, k, tn), lambda j, i: (layer, 0, j))],
        out_specs=pl.BlockSpec((tm, tn), lambda j, i: (i, j)),
        out_shape=jax.ShapeDtypeStruct((m, n), out_dtype),
        scratch_shapes=[pltpu.VMEM((k, tn), BF16)],
        compiler_params=_cp("arbitrary", "arbitrary"),
        name="matmul",
    )(a, w_stack)


def _attn_prep_kernel(p_ref, cos_ref, sin_ref, o_ref, *, daw, naw, sa, sn):
    cos = cos_ref[...]
    sin = sin_ref[...]
    lane = lax.broadcasted_iota(jnp.int32, cos.shape, 1)
    first_half = (lane % (DA_QK_DIM // 2)) < (DA_QK_DIM // 4)
    quarter = DA_QK_DIM // 4
    for j in range((3 * daw + 3 * naw) // LANES):
        c0 = j * LANES
        x = p_ref[:, c0:c0 + LANES]
        if c0 < 2 * daw:
            partner = jnp.where(first_half, pltpu.roll(x, LANES - quarter, 1), pltpu.roll(x, quarter, 1))
            x = x * cos + partner * sin
            if c0 < daw:
                x = x * sa
        elif 3 * daw <= c0 < 3 * daw + naw:
            x = x * sn
        o_ref[:, c0:c0 + LANES] = x.astype(o_ref.dtype)


def _rope_tables(n_lat, n_ctx):
    t = jnp.arange(n_lat, dtype=jnp.int32)
    rows = (t // GRID_W).astype(F32)
    cols = (t % GRID_W).astype(F32)
    n_freq = DA_QK_DIM // 4
    inv_freq = ROPE_THETA ** (-jnp.arange(n_freq, dtype=F32) / n_freq)
    ang_r = rows[:, None] * inv_freq
    ang_c = cols[:, None] * inv_freq
    ang = jnp.concatenate([ang_r, ang_r, ang_c, ang_c], axis=1)
    sign = jnp.concatenate([-jnp.ones((n_freq,), F32), jnp.ones((n_freq,), F32)] * 2)
    cos = jnp.cos(ang)
    sin = jnp.sin(ang) * sign
    reps = LANES // DA_QK_DIM
    cos = jnp.concatenate([jnp.tile(cos, (1, reps)), jnp.ones((n_ctx, LANES), F32)], axis=0)
    sin = jnp.concatenate([jnp.tile(sin, (1, reps)), jnp.zeros((n_ctx, LANES), F32)], axis=0)
    return cos, sin


def _attn_prep(p, cos, sin, daw, naw):
    t, n = p.shape
    kern = functools.partial(_attn_prep_kernel, daw=daw, naw=naw, sa=DA_QK_DIM ** -0.5 * math.log2(math.e),
                             sn=LANES ** -0.5)
    return pl.pallas_call(
        kern,
        grid=(t // ROW_TILE,),
        in_specs=[_row_spec(n), _row_spec(LANES), _row_spec(LANES)],
        out_specs=_row_spec(n),
        out_shape=jax.ShapeDtypeStruct((t, n), BF16),
        compiler_params=_cp("arbitrary"),
        name="attn_prep",
    )(p, cos, sin)


def _diff_attn_kernel(lam_ref, q_ref, k_ref, v_ref, g_ref, o_ref, qm_ref, m_ref, acc_ref, *, nk, rg):
    kv = pl.program_id(2)
    tq = q_ref.shape[0]

    @pl.when(kv == 0)
    def _():
        q = q_ref[...]
        lane = lax.broadcasted_iota(jnp.int32, q.shape, 1)
        zero = jnp.zeros_like(q)
        qm_ref[0] = jnp.where(lane < DA_QK_DIM, q, zero)
        qm_ref[1] = jnp.where(lane >= DA_QK_DIM, q, zero)
        m_ref[...] = jnp.full(m_ref.shape, NEG, F32)
        acc_ref[...] = jnp.zeros(acc_ref.shape, F32)

    k = k_ref[...]
    v = v_ref[...]
    vext = jnp.concatenate([v, jnp.ones(v.shape, v.dtype)], axis=1)
    for r0 in range(0, tq, rg):
        for mi in range(2):
            s = lax.dot_general(qm_ref[mi, r0:r0 + rg], k, (((1,), (1,)), ((), ())), preferred_element_type=F32)
            m_prev = m_ref[mi, r0:r0 + rg]
            m_new = jnp.maximum(m_prev, jnp.max(s, axis=1, keepdims=True))
            p = jnp.exp2(s - m_new)
            acc_ref[mi, r0:r0 + rg] = (jnp.exp2(m_prev - m_new) * acc_ref[mi, r0:r0 + rg]
                                       + jnp.dot(p.astype(BF16), vext, preferred_element_type=F32))
            m_ref[mi, r0:r0 + rg] = m_new

    @pl.when(kv == nk - 1)
    def _():
        a0 = acc_ref[0]
        a1 = acc_ref[1]
        o = a0[:, :LANES] / a0[:, LANES:LANES + 1] - lam_ref[0] * (a1[:, :LANES] / a1[:, LANES:LANES + 1])
        o_ref[...] = (_rms(o, g_ref[...]) * lam_ref[1]).astype(o_ref.dtype)


DA_TQ, DA_TK, DA_RG = 1024, 2816, 256


def _diff_attn(qkv, lam2, subln_g, q_row0, n_q, k_row0, n_k, heads):
    tq = _pick(n_q, (DA_TQ, 1024, 512, 256))
    tk = _pick(n_k, (DA_TK, 1408, 1280, 1024, 768, 512, 256))
    assert q_row0 % tq == 0 and k_row0 % tk == 0
    nq, nk = n_q // tq, n_k // tk
    qb, kb = q_row0 // tq, k_row0 // tk
    return pl.pallas_call(
        functools.partial(_diff_attn_kernel, nk=nk, rg=min(DA_RG, tq)),
        grid=(heads, nq, nk),
        in_specs=[pl.BlockSpec(memory_space=pltpu.SMEM),
                  pl.BlockSpec((tq, LANES), lambda h, i, j: (qb + i, h)),
                  pl.BlockSpec((tk, LANES), lambda h, i, j: (kb + j, heads + h)),
                  pl.BlockSpec((tk, LANES), lambda h, i, j: (kb + j, 2 * heads + h)),
                  pl.BlockSpec((1, LANES), lambda h, i, j: (0, 0))],
        out_specs=pl.BlockSpec((tq, LANES), lambda h, i, j: (i, h)),
        out_shape=jax.ShapeDtypeStruct((n_q, heads * LANES), BF16),
        scratch_shapes=[pltpu.VMEM((2, tq, LANES), BF16), pltpu.VMEM((2, tq, 1), F32),
                        pltpu.VMEM((2, tq, 2 * LANES), F32)],
        compiler_params=_cp("arbitrary", "arbitrary", "arbitrary"),
        name="diff_attn",
    )(lam2, qkv, qkv, qkv, subln_g.reshape(1, LANES))


def _na_bias_tables(rpb, rows):
    wr, wc = NA_WIN_ROWS, NA_WIN_COLS
    heads = rpb.shape[0]
    c = np.arange(GRID_W)[:, None]
    kc = np.arange(GRID_W)[None, :]
    cs = np.clip(c - wc // 2, 0, GRID_W - wc)
    cvalid = (kc >= cs) & (kc < cs + wc)
    onehot = ((kc - c + wc - 1)[None] == np.arange(2 * wc - 1)[:, None, None]) & cvalid[None]
    sel = jnp.asarray(onehot.reshape(2 * wc - 1, GRID_W * GRID_W), F32)
    toe = jnp.dot(rpb.astype(F32).reshape(heads * (2 * wr - 1), 2 * wc - 1), sel,
                  precision=lax.Precision.HIGHEST)
    toe = toe.reshape(heads, 2 * wr - 1, GRID_W, GRID_W) + jnp.asarray(np.where(cvalid, 0.0, NEG), F32)
    toe = jnp.concatenate([toe, jnp.full((heads, 1, GRID_W, GRID_W), NEG, F32)], axis=1)
    r0s = np.array([0, NA_QROWS, rows - NA_QROWS])
    ks = np.clip(r0s - wr // 2, 0, rows - NA_KROWS)
    r = r0s[:, None, None] + np.arange(NA_QROWS)[None, :, None]
    kr = ks[:, None, None] + np.arange(NA_KROWS)[None, None, :]
    rs = np.clip(r - wr // 2, 0, rows - wr)
    idx = np.where((kr >= rs) & (kr < rs + wr), kr - r + wr - 1, 2 * wr - 1)
    blocks = toe[:, idx]
    return blocks.transpose(0, 1, 2, 4, 3, 5).reshape(heads, 3, NA_QROWS * GRID_W, NA_KROWS * GRID_W)


def _na_kernel(q_ref, k_ref, v_ref, b_ref, o_ref, *, n_lat, n_ctx, rows):
    rb = pl.program_id(1)
    n_rb = rows // NA_QROWS
    q = q_ref[...]
    kc = k_ref[n_lat:n_lat + n_ctx, :]
    vc = v_ref[n_lat:n_lat + n_ctx, :]
    nt = (((1,), (1,)), ((), ()))
    s_ctx = lax.dot_general(q, kc, nt, preferred_element_type=F32)

    @pl.when(rb < n_rb)
    def _():
        ks = jnp.clip(rb * NA_QROWS - NA_WIN_ROWS // 2, 0, rows - NA_KROWS)
        start = pl.multiple_of(ks * GRID_W, GRID_W)
        kw = k_ref[pl.ds(start, NA_KROWS * GRID_W), :]
        vw = v_ref[pl.ds(start, NA_KROWS * GRID_W), :]
        s_loc = lax.dot_general(q, kw, nt, preferred_element_type=F32) + b_ref[0, 0]
        m = jnp.maximum(jnp.max(s_loc, axis=1, keepdims=True), jnp.max(s_ctx, axis=1, keepdims=True))
        p_loc = jnp.exp(s_loc - m)
        p_ctx = jnp.exp(s_ctx - m)
        l = jnp.sum(p_loc, axis=1, keepdims=True) + jnp.sum(p_ctx, axis=1, keepdims=True)
        o = (jnp.dot(p_ctx.astype(BF16), vc, preferred_element_type=F32)
             + jnp.dot(p_loc.astype(BF16), vw, preferred_element_type=F32))
        o_ref[...] = (o / l).astype(o_ref.dtype)

    @pl.when(rb == n_rb)
    def _():
        m = jnp.max(s_ctx, axis=1, keepdims=True)
        p = jnp.exp(s_ctx - m)
        l = jnp.sum(p, axis=1, keepdims=True)
        o = jnp.dot(p.astype(BF16), vc, preferred_element_type=F32)
        o_ref[...] = (o / l).astype(o_ref.dtype)


def _na_attn(qkv, bias, n_lat, n_ctx, da_heads, heads):
    t = qkv.shape[0]
    rows = n_lat // GRID_W
    n_rb = rows // NA_QROWS
    tq = NA_QROWS * GRID_W
    assert tq == ROW_TILE and n_ctx == ROW_TILE
    q0, k0, v0 = 3 * da_heads, 3 * da_heads + heads, 3 * da_heads + 2 * heads

    def bias_map(h, rb):
        return (h, jnp.where(rb == 0, 0, jnp.where(rb == n_rb - 1, 2, 1)), 0, 0)

    return pl.pallas_call(
        functools.partial(_na_kernel, n_lat=n_lat, n_ctx=n_ctx, rows=rows),
        grid=(heads, n_rb + 1),
        in_specs=[pl.BlockSpec((tq, LANES), lambda h, rb: (rb, q0 + h)),
                  pl.BlockSpec((t, LANES), lambda h, rb: (0, k0 + h)),
                  pl.BlockSpec((t, LANES), lambda h, rb: (0, v0 + h)),
                  pl.BlockSpec((1, 1, tq, NA_KROWS * GRID_W), bias_map)],
        out_specs=pl.BlockSpec((tq, LANES), lambda h, rb: (rb, h)),
        out_shape=jax.ShapeDtypeStruct((t, heads * LANES), BF16),
        compiler_params=_cp("arbitrary", "arbitrary"),
        name="na_attn",
    )(qkv, qkv, qkv, bias)


def _hgrn_prep(q, v, z, lb, rev):
    c = HG_CHUNK
    f = lb + (1.0 - lb) * jax.nn.sigmoid(z)
    kk = 1.0 - f
    g = jnp.log2(f)
    ri = lax.broadcasted_iota(jnp.int32, (c, c), 0)
    ci = lax.broadcasted_iota(jnp.int32, (c, c), 1)
    tri = ((ri <= ci) if rev else (ri >= ci)).astype(BF16)
    g_hi = g.astype(BF16)
    r1 = g - g_hi.astype(F32)
    g_mid = r1.astype(BF16)
    g_lo = (r1 - g_mid.astype(F32)).astype(BF16)
    c2 = jnp.dot(tri, jnp.concatenate([g_hi, g_mid], axis=1), preferred_element_type=F32)
    cum = c2[:, :LANES] + c2[:, LANES:] + jnp.dot(tri, g_lo, preferred_element_type=F32)
    total = jnp.sum(g, axis=0, keepdims=True)
    return dict(cum=cum, total=total, kk=kk, qs=_silu(q) * (LANES ** -0.5), v=v, vb=v.astype(BF16))


def _hgrn_intra(pp, rev):
    c, sub, nsub = HG_CHUNK, HG_SUB, HG_CHUNK // HG_SUB
    nt = (((1,), (1,)), ((), ()))
    cum, kk, qs, v, vb = pp["cum"], pp["kk"], pp["qs"], pp["v"], pp["vb"]
    ngrp = sub // SUBLANES
    ti = lax.broadcasted_iota(jnp.int32, (SUBLANES, LANES), 0)
    ckey = cum - jnp.log2(kk)
    o_blocks = [None] * nsub
    for i in range(nsub):
        lo = (nsub - 1 - i) * sub if rev else i * sub
        hi = lo + sub
        cum_i, q_i = cum[lo:hi], qs[lo:hi]
        og = [None] * ngrp
        if i > 0:
            brow = hi if rev else lo - 1
            bnd = cum[brow:brow + 1]
            elo, ehi = (hi, c) if rev else (0, lo)
            qt = (q_i * jnp.exp2(cum_i - bnd)).astype(BF16)
            kt = (kk[elo:ehi] * jnp.exp2(bnd - cum[elo:ehi])).astype(BF16)
            att = lax.dot_general(qt, kt, nt, preferred_element_type=F32)
            o_i = jnp.dot(att.astype(BF16), vb[elo:ehi], preferred_element_type=F32)
            og = [o_i[gq * SUBLANES:(gq + 1) * SUBLANES] for gq in range(ngrp)]
        zs, dst = [], []
        for s in range(sub):
            gs = s // SUBLANES
            for gq in (range(gs + 1) if rev else range(gs, ngrp)):
                r = slice(gq * SUBLANES, (gq + 1) * SUBLANES)
                dlt = cum_i[r] - ckey[lo + s:lo + s + 1]
                if gq == gs:
                    dlt = jnp.where((ti <= s - r.start) if rev else (ti >= s - r.start), dlt, NEG)
                zs.append(jnp.exp2(dlt) * q_i[r])
                dst.append((gq, s))
        a_all = jnp.sum(jnp.concatenate(zs, axis=0), axis=1, keepdims=True)
        for n, (gq, s) in enumerate(dst):
            term = a_all[n * SUBLANES:(n + 1) * SUBLANES] * v[lo + s:lo + s + 1]
            og[gq] = term if og[gq] is None else og[gq] + term
        o_blocks[(nsub - 1 - i) if rev else i] = jnp.concatenate(og, axis=0)
    return jnp.concatenate(o_blocks, axis=0)


def _hgrn_carry(pp, st):
    cum, total = pp["cum"], pp["total"]
    o_inter = lax.dot_general((pp["qs"] * jnp.exp2(cum)).astype(BF16), st.astype(BF16),
                              (((1,), (1,)), ((), ())), preferred_element_type=F32)
    kdec = (pp["kk"] * jnp.exp2(total - cum)).astype(BF16)
    st_new = st * jnp.exp2(total) + lax.dot_general(pp["vb"], kdec, (((0,), (0,)), ((), ())),
                                                    preferred_element_type=F32)
    return o_inter, st_new


def _hgrn_kernel(*refs, rev, final, n_chunks):
    if final:
        q_ref, v_ref, z_ref, lb_ref, of_ref, gate_ref, gn_ref, o_ref, st_ref = refs
    else:
        q_ref, v_ref, z_ref, lb_ref, o_ref, st_ref = refs

    @pl.when(pl.program_id(1) == 0)
    def _():
        st_ref[...] = jnp.zeros(st_ref.shape, F32)

    order = [(n_chunks - 1 - s) if rev else s for s in range(n_chunks)]
    tiles = [(slice(ch * HG_CHUNK, (ch + 1) * HG_CHUNK), hh, slice(hh * LANES, (hh + 1) * LANES))
             for ch in order for hh in range(HG_HPS)]
    preps = [_hgrn_prep(q_ref[rows, cols], v_ref[rows, cols], z_ref[rows, cols], lb_ref[:, cols], rev)
             for rows, _, cols in tiles]
    intras = [_hgrn_intra(pp, rev) for pp in preps]
    states = [st_ref[hh] for hh in range(HG_HPS)]
    for (rows, hh, cols), pp, o in zip(tiles, preps, intras):
        o_inter, states[hh] = _hgrn_carry(pp, states[hh])
        o = o + o_inter
        if final:
            o = _rms(o + of_ref[rows, cols], gn_ref[...]) * _silu(gate_ref[rows, cols])
        o_ref[rows, cols] = o.astype(o_ref.dtype)
    for hh in range(HG_HPS):
        st_ref[hh] = states[hh]


def _hgrn_dir(p, lb_dir, rev, n_lat, heads, o_fwd=None, gnorm_g=None):
    t = p.shape[0]
    d = heads * LANES
    n_blk = t // ROW_TILE
    last = n_blk - 1
    final = o_fwd is not None

    def tok(j):
        return jnp.where(j == 0, last, (last - j) if rev else (j - 1))

    assert heads % HG_HPS == 0
    hg = heads // HG_HPS
    w = HG_HPS * LANES
    zcol = 3 * hg if rev else 2 * hg
    in_specs = [pl.BlockSpec((ROW_TILE, w), lambda h, j: (tok(j), h)),
                pl.BlockSpec((ROW_TILE, w), lambda h, j: (tok(j), hg + h)),
                pl.BlockSpec((ROW_TILE, w), lambda h, j: (tok(j), zcol + h)),
                pl.BlockSpec((1, w), lambda h, j: (0, h))]
    args = [p, p, p, lb_dir.reshape(1, d)]
    if final:
        in_specs += [pl.BlockSpec((ROW_TILE, w), lambda h, j: (tok(j), h)),
                     pl.BlockSpec((ROW_TILE, w), lambda h, j: (tok(j), 4 * hg + h)),
                     pl.BlockSpec((1, LANES), lambda h, j: (0, 0))]
        args += [o_fwd, p, gnorm_g.reshape(1, LANES)]
    return pl.pallas_call(
        functools.partial(_hgrn_kernel, rev=rev, final=final, n_chunks=ROW_TILE // HG_CHUNK),
        grid=(hg, n_blk),
        in_specs=in_specs,
        out_specs=pl.BlockSpec((ROW_TILE, w), lambda h, j: (tok(j), h)),
        out_shape=jax.ShapeDtypeStruct((t, d), BF16 if final else F32),
        scratch_shapes=[pltpu.VMEM((HG_HPS, LANES, LANES), F32)],
        compiler_params=_cp("arbitrary", "arbitrary"),
        name="hgrn_bwd" if rev else "hgrn_fwd",
    )(*args)


def _new_expert(be_ref):
    b = pl.program_id(0)
    return jnp.logical_or(b == 0, be_ref[b] != be_ref[jnp.maximum(b - 1, 0)])


def _moe_up_kernel(be_ref, nu_ref, base_ref, nval_ref, tok_ref, x_hbm, wg_ref, wu_ref, o_ref,
                   wgb_ref, wub_ref, xbuf, sem):
    b = pl.program_id(0)
    nch = wg_ref.shape[2] // LANES

    def gather(blk, start):
        base = base_ref[blk]
        last = jnp.maximum(nval_ref[blk] - 1, 0)
        _row_gather(lambda r: tok_ref[base + jnp.minimum(r, last)], x_hbm, xbuf.at[blk % 2], sem.at[blk % 2],
                    MOE_BM, start, span=nch)

    @pl.when(b == 0)
    def _():
        gather(b, True)

    @pl.when(b + 1 < nu_ref[0])
    def _():
        gather(b + 1, True)

    @pl.when(jnp.logical_and(b < nu_ref[0], _new_expert(be_ref)))
    def _():
        wgb_ref[...] = wg_ref[0, 0].astype(BF16)
        wub_ref[...] = wu_ref[0, 0].astype(BF16)

    @pl.when(b < nu_ref[0])
    def _():
        gather(b, False)
        xb = xbuf.at[b % 2]
        x = jnp.concatenate([xb[pl.ds(j, MOE_BM, stride=nch), :] for j in range(nch)], axis=1).astype(BF16)
        hg = jnp.dot(x, wgb_ref[...], preferred_element_type=F32)
        hu = jnp.dot(x, wub_ref[...], preferred_element_type=F32)
        o_ref[...] = (_silu(hg) * hu).astype(o_ref.dtype)

    @pl.when(b >= nu_ref[0])
    def _():
        o_ref[...] = jnp.zeros(o_ref.shape, o_ref.dtype)


def _moe_down_kernel(be_ref, nu_ref, h_ref, wd_ref, o_ref, wdb_ref):
    b = pl.program_id(0)

    @pl.when(jnp.logical_and(b < nu_ref[0], _new_expert(be_ref)))
    def _():
        wdb_ref[...] = wd_ref[0, 0].astype(BF16)

    @pl.when(b < nu_ref[0])
    def _():
        o_ref[...] = jnp.dot(h_ref[...], wdb_ref[...], preferred_element_type=F32)

    @pl.when(b >= nu_ref[0])
    def _():
        o_ref[...] = jnp.zeros(o_ref.shape, o_ref.dtype)


def _moe_experts(m, tok_sorted, blk_expert, blk_base, blk_nval, n_used, w_gate, w_up, w_down, layer):
    d = w_gate.shape[2]
    n_blk = blk_expert.shape[0]
    n_slot = n_blk * MOE_BM
    de = w_gate.shape[3]
    hid = pl.pallas_call(
        _moe_up_kernel,
        grid_spec=pltpu.PrefetchScalarGridSpec(
            num_scalar_prefetch=5, grid=(n_blk,),
            in_specs=[pl.BlockSpec(memory_space=pl.ANY),
                      pl.BlockSpec((1, 1, d, de), lambda b, be, *_: (layer, be[b], 0, 0)),
                      pl.BlockSpec((1, 1, d, de), lambda b, be, *_: (layer, be[b], 0, 0))],
            out_specs=pl.BlockSpec((MOE_BM, de), lambda b, *_: (b, 0)),
            scratch_shapes=[pltpu.VMEM((d, de), BF16), pltpu.VMEM((d, de), BF16),
                            pltpu.VMEM((2, MOE_BM * (d // LANES), LANES), F32), pltpu.SemaphoreType.DMA((2,))]),
        out_shape=jax.ShapeDtypeStruct((n_slot, de), BF16),
        compiler_params=_cp("arbitrary"),
        name="moe_up",
    )(blk_expert, n_used, blk_base, blk_nval, tok_sorted, m, w_gate, w_up)
    return pl.pallas_call(
        _moe_down_kernel,
        grid_spec=pltpu.PrefetchScalarGridSpec(
            num_scalar_prefetch=2, grid=(n_blk,),
            in_specs=[pl.BlockSpec((MOE_BM, de), lambda b, be, nu: (b, 0)),
                      pl.BlockSpec((1, 1, de, d), lambda b, be, nu: (layer, be[b], 0, 0))],
            out_specs=pl.BlockSpec((MOE_BM, d), lambda b, be, nu: (b, 0)),
            scratch_shapes=[pltpu.VMEM((de, d), BF16)]),
        out_shape=jax.ShapeDtypeStruct((n_slot, d), F32),
        compiler_params=_cp("arbitrary"),
        name="moe_down",
    )(blk_expert, n_used, hid, w_down)


def _route(logits):
    n_exp = N_GROUPS * EXPERTS_PER_GROUP
    t = logits.shape[0]
    g_logits = logits[:, :N_GROUPS]
    g_prob = jax.nn.softmax(g_logits, axis=-1)
    g_sel = jnp.argmax(g_logits, axis=-1).astype(jnp.int32)
    g_w = jnp.take_along_axis(g_prob, g_sel[:, None], axis=1)[:, 0]
    e_logits = logits[:, N_GROUPS:N_GROUPS + n_exp].reshape(t, N_GROUPS, EXPERTS_PER_GROUP)
    e_logits = jnp.take_along_axis(e_logits, g_sel[:, None, None], axis=1)[:, 0]
    top_v, top_i = lax.top_k(e_logits, TOP_K)
    weights = g_w[:, None] * jax.nn.softmax(top_v, axis=-1)
    expert = g_sel[:, None] * EXPERTS_PER_GROUP + top_i.astype(jnp.int32)
    return expert, weights


def _dispatch(expert, weights):
    n_exp = N_GROUPS * EXPERTS_PER_GROUP
    t = expert.shape[0]
    n_as = t * TOP_K
    eid = expert.reshape(-1)
    aidx = jnp.arange(n_as, dtype=jnp.int32)
    eid_s, order = lax.sort((eid, aidx), num_keys=1, is_stable=True)
    onehot = eid_s[:, None] == jnp.arange(n_exp, dtype=jnp.int32)[None, :]
    counts = jnp.sum(onehot, axis=0, dtype=jnp.int32)
    padded = (counts + MOE_BM - 1) // MOE_BM * MOE_BM
    start = jnp.cumsum(counts) - counts
    pend = jnp.cumsum(padded)
    pstart = pend - padded
    dest = aidx + jnp.sum(jnp.where(onehot, (pstart - start)[None, :], 0), axis=1)
    n_blk = -(-n_as // MOE_BM) + n_exp
    blk_expert = jnp.searchsorted(pend, jnp.arange(n_blk, dtype=jnp.int32) * MOE_BM, side='right')
    blk_expert = jnp.minimum(blk_expert, n_exp - 1).astype(jnp.int32)
    n_used = (pend[-1] // MOE_BM).astype(jnp.int32).reshape(1)
    blk_first = jnp.arange(n_blk, dtype=jnp.int32) * MOE_BM - pstart[blk_expert]
    blk_base = jnp.clip(start[blk_expert] + blk_first, 0, n_as - 1).astype(jnp.int32)
    blk_nval = jnp.clip(counts[blk_expert] - blk_first, 0, MOE_BM).astype(jnp.int32)
    blk_base = jnp.minimum(blk_base, n_as - jnp.maximum(blk_nval, 1))
    _, slot_of = lax.sort((order, dest), num_keys=1)
    slot_idx = slot_of.reshape(t, TOP_K).T.reshape(-1)
    return order // TOP_K, blk_expert, blk_base, blk_nval, n_used, slot_idx


def _moe(m, logits, w_gate, w_up, w_down, layer):
    expert, weights = _route(logits)
    tok_sorted, blk_expert, blk_base, blk_nval, n_used, slot_idx = _dispatch(expert, weights)
    ys = _moe_experts(m, tok_sorted, blk_expert, blk_base, blk_nval, n_used, w_gate, w_up, w_down, layer)
    return ys, slot_idx, weights.T[:, :, None]


def _diff_lambda(lam_p, layer_idx):
    lam_init = 0.8 - 0.6 * math.exp(-0.3 * layer_idx)
    lf = lam_p.astype(F32)
    lam = jnp.exp(jnp.sum(lf[0] * lf[1])) - jnp.exp(jnp.sum(lf[2] * lf[3])) + lam_init
    return jnp.stack([lam, jnp.asarray(1.0 - lam_init, F32)]).astype(F32)


def kernel(x, c, ctx, c_ctx, w_mod, b_mod, norm1_g, norm2_g, att_w_in, att_w_out, att_lambda, att_subln_g,
           att_rpb, rec_w_in, rec_w_out, rec_lb_logits, rec_gnorm_g, moe_w_group, moe_b_group, moe_w_router,
           moe_b_router, moe_w_gate, moe_w_up, moe_w_down, final_norm_g):
    b_, n_lat, d = x.shape
    n_ctx = ctx.shape[1]
    assert b_ == 1 and n_ctx == ROW_TILE and n_lat % ROW_TILE == 0 and d % LANES == 0
    depth = w_mod.shape[0]
    daw, naw = DA_HEADS * LANES, NA_HEADS * LANES
    n_exp = N_GROUPS * EXPERTS_PER_GROUP

    lbp = jax.nn.softmax(rec_lb_logits.astype(F32), axis=0)
    lbs = jnp.cumsum(lbp, axis=0) - lbp[0:1]

    cvec = jnp.zeros((8, d), F32).at[0].set(c[0]).at[1].set(c_ctx)
    mods = _modulation(cvec, w_mod, b_mod)
    cos, sin = _rope_tables(n_lat, n_ctx)

    h = jnp.concatenate([x[0], ctx[0]], axis=0)
    a = _ln_mod(h, norm1_g[0], mods[0], 0, 1, n_lat)
    out = None
    for l in range(depth):
        j = l // 2
        mod = mods[l]
        if l % 2 == 0:
            p = _matmul(a, att_w_in, j, F32)
            qkv = _attn_prep(p, cos, sin, daw, naw)
            lam2 = _diff_lambda(att_lambda[j], l)
            da_lat = _diff_attn(qkv, lam2, att_subln_g[j], 0, n_lat, 0, n_lat + n_ctx, DA_HEADS)
            da_ctx = _diff_attn(qkv, lam2, att_subln_g[j], n_lat, n_ctx, n_lat, n_ctx, DA_HEADS)
            bias = _na_bias_tables(att_rpb[j], n_lat // GRID_W)
            na = _na_attn(qkv, bias, n_lat, n_ctx, DA_HEADS, NA_HEADS)
            cat = jnp.concatenate([jnp.concatenate([da_lat, da_ctx], axis=0), na], axis=1)
            y = _matmul(cat, att_w_out, j, F32)
        else:
            p = _matmul(a, rec_w_in, j, F32)
            o_f = _hgrn_dir(p, lbs[l, 0], False, n_lat, HG_HEADS)
            o = _hgrn_dir(p, lbs[l, 1], True, n_lat, HG_HEADS, o_fwd=o_f, gnorm_g=rec_gnorm_g[j])
            y = _matmul(o, rec_w_out, j, F32)
        w_route = jnp.zeros((d, LANES), F32).at[:, :N_GROUPS].set(moe_w_group[l])
        w_route = w_route.at[:, N_GROUPS:N_GROUPS + n_exp].set(moe_w_router[l])
        b_route = jnp.zeros((1, LANES), F32).at[0, :N_GROUPS].set(moe_b_group[l])
        b_route = b_route.at[0, N_GROUPS:N_GROUPS + n_exp].set(moe_b_router[l])
        h, m, logits = _res_ln_mod_route(h, y, norm2_g[l], mod, 2, 3, 4, w_route, b_route, n_lat)
        ys, slot_idx, wt = _moe(m, logits, moe_w_gate, moe_w_up, moe_w_down, l)
        if l + 1 < depth:
            h, a = _res_ln_mod(h, ys, slot_idx, wt, norm1_g[l + 1], mod, 5, mods[l + 1], 0, 1, n_lat)
        else:
            out = _res_final(h, ys, slot_idx, wt, final_norm_g, mod, 5, n_lat)
    return out[None]
```

```python
import functools
import math

import numpy as np
import jax
import jax.numpy as jnp
from jax import lax
from jax.experimental import pallas as pl
from jax.experimental.pallas import tpu as pltpu

F32 = jnp.float32
BF16 = jnp.bfloat16

GRID_W = 64
EPS = 1e-6
DA_HEADS = 8
DA_QK_DIM = 64
NA_HEADS = 8
NA_WIN_ROWS = 8
NA_WIN_COLS = 16
ROPE_THETA = 10000.0
HG_HEADS = 16
N_GROUPS = 4
EXPERTS_PER_GROUP = 8
TOP_K = 2

LANES = 128
SUBLANES = 8
ROW_TILE = 256
HG_CHUNK = 64
HG_SUB = 32
HG_HPS = 2
NA_QROWS = 4
NA_KROWS = 12
NA_HPS = 4
MOE_BM = 256
NEG = -1e30
VMEM_LIMIT = 56 << 20


def _cp(*sem):
    return pltpu.CompilerParams(dimension_semantics=sem, vmem_limit_bytes=VMEM_LIMIT)


def _pick(n, cands):
    for c in cands:
        if n % c == 0:
            return c
    raise ValueError(f"no tile for {n} in {cands}")


def _silu(x):
    return x * jax.nn.sigmoid(x)


def _rms(x, g):
    return x * lax.rsqrt(jnp.mean(x * x, axis=-1, keepdims=True) + EPS) * g


def _mod_kernel(c_ref, w_ref, b_ref, o_ref):
    s = _silu(c_ref[...]).astype(BF16)
    o_ref[0] = jnp.dot(s, w_ref[0].astype(BF16), preferred_element_type=F32) + b_ref[0]


def _modulation(cvec, w_mod, b_mod):
    depth, d, n = w_mod.shape
    tn = _pick(n, (1024, 512, 256, 128))
    return pl.pallas_call(
        _mod_kernel,
        grid=(depth, n // tn),
        in_specs=[pl.BlockSpec((8, d), lambda l, j: (0, 0)),
                  pl.BlockSpec((1, d, tn), lambda l, j: (l, 0, j)),
                  pl.BlockSpec((1, 1, tn), lambda l, j: (l, 0, j))],
        out_specs=pl.BlockSpec((1, 8, tn), lambda l, j: (l, 0, j)),
        out_shape=jax.ShapeDtypeStruct((depth, 8, n), F32),
        compiler_params=_cp("arbitrary", "arbitrary"),
        name="modulation",
    )(cvec, w_mod, b_mod.reshape(depth, 1, n))


def _mod_row(ref, n_lat_tiles):
    r = (pl.program_id(0) >= n_lat_tiles).astype(jnp.int32)
    return ref[pl.ds(r, 1), :]


def _ln_mod_kernel(h_ref, g_ref, sh_ref, sc_ref, a_ref, *, n_lat_tiles):
    y = _rms(h_ref[...], g_ref[...])
    a = y * (1.0 + _mod_row(sc_ref, n_lat_tiles)) + _mod_row(sh_ref, n_lat_tiles)
    a_ref[...] = a.astype(a_ref.dtype)


def _row_gather(row_of, src_hbm, dst, sem, n_rows, start, span=1):
    def body(r, carry):
        cp = pltpu.make_async_copy(src_hbm.at[pl.ds(pl.multiple_of(row_of(r) * span, span), span)],
                                   dst.at[pl.ds(pl.multiple_of(r * span, span), span)], sem)
        if start:
            cp.start()
        else:
            cp.wait()
        return carry

    lax.fori_loop(0, n_rows, body, 0, unroll=8)


def _combined_expert_rows(idx_ref, wt_ref, ys_hbm, ybuf, sem, t):
    i = pl.program_id(0)

    def gather(tile, start):
        slot = tile % 2
        for k in range(TOP_K):
            base = k * t + tile * ROW_TILE
            _row_gather(lambda r: idx_ref[base + r], ys_hbm, ybuf.at[slot, k], sem.at[slot], ROW_TILE, start)

    @pl.when(i == 0)
    def _():
        gather(i, True)

    @pl.when(i + 1 < pl.num_programs(0))
    def _():
        gather(i + 1, True)

    gather(i, False)
    y = wt_ref[0] * ybuf[i % 2, 0]
    for k in range(1, TOP_K):
        y = y + wt_ref[k] * ybuf[i % 2, k]
    return y


def _res_ln_mod_kernel(idx_ref, h_ref, wt_ref, ys_hbm, gt_ref, g_ref, sh_ref, sc_ref, ho_ref, a_ref, ybuf, sem,
                       *, n_lat_tiles, t):
    y = _combined_expert_rows(idx_ref, wt_ref, ys_hbm, ybuf, sem, t)
    h = h_ref[...] + _mod_row(gt_ref, n_lat_tiles) * y
    ho_ref[...] = h
    a = _rms(h, g_ref[...]) * (1.0 + _mod_row(sc_ref, n_lat_tiles)) + _mod_row(sh_ref, n_lat_tiles)
    a_ref[...] = a.astype(a_ref.dtype)


def _res_ln_mod_route_kernel(h_ref, y_ref, gt_ref, g_ref, sh_ref, sc_ref, wr_ref, br_ref,
                             ho_ref, a_ref, lg_ref, *, n_lat_tiles):
    h = h_ref[...] + _mod_row(gt_ref, n_lat_tiles) * y_ref[...]
    ho_ref[...] = h
    a = _rms(h, g_ref[...]) * (1.0 + _mod_row(sc_ref, n_lat_tiles)) + _mod_row(sh_ref, n_lat_tiles)
    nch = a.shape[1] // LANES
    for j in range(nch):
        a_ref[pl.ds(j, ROW_TILE, stride=nch), :] = a[:, j * LANES:(j + 1) * LANES]
    lg_ref[...] = jnp.dot(a, wr_ref[...], preferred_element_type=F32,
                          precision=lax.Precision.HIGHEST) + br_ref[...]


def _res_final_kernel(idx_ref, h_ref, wt_ref, ys_hbm, gt_ref, g_ref, o_ref, ybuf, sem, *, t):
    y = _combined_expert_rows(idx_ref, wt_ref, ys_hbm, ybuf, sem, t)
    h = h_ref[...] + gt_ref[pl.ds(0, 1), :] * y
    o_ref[...] = _rms(h, g_ref[...])


def _row_spec(d):
    return pl.BlockSpec((ROW_TILE, d), lambda i, *_: (i, 0))


def _vec_spec(d):
    return pl.BlockSpec((1, d), lambda i, *_: (0, 0))


def _mod_spec(d, k):
    return pl.BlockSpec((8, d), lambda i, *_: (0, k))


def _combine_scratch(d):
    return [pltpu.VMEM((2, TOP_K, ROW_TILE, d), F32), pltpu.SemaphoreType.DMA((2,))]


def _wt_spec():
    return pl.BlockSpec((TOP_K, ROW_TILE, 1), lambda i, *_: (0, i, 0))


def _ln_mod(h, g, mod, k_shift, k_scale, n_lat):
    t, d = h.shape
    return pl.pallas_call(
        functools.partial(_ln_mod_kernel, n_lat_tiles=n_lat // ROW_TILE),
        grid=(t // ROW_TILE,),
        in_specs=[_row_spec(d), _vec_spec(d), _mod_spec(d, k_shift), _mod_spec(d, k_scale)],
        out_specs=_row_spec(d),
        out_shape=jax.ShapeDtypeStruct((t, d), BF16),
        compiler_params=_cp("arbitrary"),
        name="ln_mod",
    )(h, g.reshape(1, d), mod, mod)


def _res_ln_mod(h, ys, slot_idx, wt, g, mod_gate, k_gate, mod_next, k_shift, k_scale, n_lat):
    t, d = h.shape
    return pl.pallas_call(
        functools.partial(_res_ln_mod_kernel, n_lat_tiles=n_lat // ROW_TILE, t=t),
        grid_spec=pltpu.PrefetchScalarGridSpec(
            num_scalar_prefetch=1, grid=(t // ROW_TILE,),
            in_specs=[_row_spec(d), _wt_spec(), pl.BlockSpec(memory_space=pl.ANY), _mod_spec(d, k_gate),
                      _vec_spec(d), _mod_spec(d, k_shift), _mod_spec(d, k_scale)],
            out_specs=[_row_spec(d), _row_spec(d)],
            scratch_shapes=_combine_scratch(d)),
        out_shape=[jax.ShapeDtypeStruct((t, d), F32), jax.ShapeDtypeStruct((t, d), BF16)],
        compiler_params=_cp("arbitrary"),
        name="res_ln_mod",
    )(slot_idx, h, wt, ys, mod_gate, g.reshape(1, d), mod_next, mod_next)


def _res_ln_mod_route(h, y, g, mod, k_gate, k_shift, k_scale, w_route, b_route, n_lat):
    t, d = h.shape
    return pl.pallas_call(
        functools.partial(_res_ln_mod_route_kernel, n_lat_tiles=n_lat // ROW_TILE),
        grid=(t // ROW_TILE,),
        in_specs=[_row_spec(d), _row_spec(d), _mod_spec(d, k_gate), _vec_spec(d),
                  _mod_spec(d, k_shift), _mod_spec(d, k_scale),
                  pl.BlockSpec((d, LANES), lambda i: (0, 0)), _vec_spec(LANES)],
        out_specs=[_row_spec(d), pl.BlockSpec((ROW_TILE * (d // LANES), LANES), lambda i: (i, 0)),
                   _row_spec(LANES)],
        out_shape=[jax.ShapeDtypeStruct((t, d), F32), jax.ShapeDtypeStruct((t * (d // LANES), LANES), F32),
                   jax.ShapeDtypeStruct((t, LANES), F32)],
        compiler_params=_cp("arbitrary"),
        name="res_ln_mod_route",
    )(h, y, mod, g.reshape(1, d), mod, mod, w_route, b_route)


def _res_final(h, ys, slot_idx, wt, g, mod, k_gate, n_lat):
    t, d = h.shape
    return pl.pallas_call(
        functools.partial(_res_final_kernel, t=t),
        grid_spec=pltpu.PrefetchScalarGridSpec(
            num_scalar_prefetch=1, grid=(n_lat // ROW_TILE,),
            in_specs=[_row_spec(d), _wt_spec(), pl.BlockSpec(memory_space=pl.ANY), _mod_spec(d, k_gate),
                      _vec_spec(d)],
            out_specs=_row_spec(d),
            scratch_shapes=_combine_scratch(d)),
        out_shape=jax.ShapeDtypeStruct((n_lat, d), F32),
        compiler_params=_cp("arbitrary"),
        name="res_final",
    )(slot_idx, h, wt, ys, mod, g.reshape(1, d))


def _matmul_kernel(a_ref, w_ref, o_ref, wb_ref):
    @pl.when(pl.program_id(1) == 0)
    def _():
        wb_ref[...] = w_ref[0].astype(BF16)

    o_ref[...] = jnp.dot(a_ref[...], wb_ref[...], preferred_element_type=F32).astype(o_ref.dtype)


def _matmul(a, w_stack, layer, out_dtype):
    m, k = a.shape
    n = w_stack.shape[2]
    tm = _pick(m, (768, 640, 512, 256))
    tn = _pick(n, (1024, 512, 256, 128))
    return pl.pallas_call(
        _matmul_kernel,
        grid=(n // tn, m // tm),
        in_specs=[pl.BlockSpec((tm, k), lambda j, i: (i, 0)),
                  pl.BlockSpec((1, k, tn), lambda j, i: (layer, 0, j))],
        out_specs=pl.BlockSpec((tm, tn), lambda j, i: (i, j)),
        out_shape=jax.ShapeDtypeStruct((m, n), out_dtype),
        scratch_shapes=[pltpu.VMEM((k, tn), BF16)],
        compiler_params=_cp("arbitrary", "arbitrary"),
        name="matmul",
    )(a, w_stack)


def _attn_prep_kernel(p_ref, cos_ref, sin_ref, o_ref, *, daw, naw, sa, sn):
    cos = cos_ref[...]
    sin = sin_ref[...]
    lane = lax.broadcasted_iota(jnp.int32, cos.shape, 1)
    first_half = (lane % (DA_QK_DIM // 2)) < (DA_QK_DIM // 4)
    quarter = DA_QK_DIM // 4
    for j in range((3 * daw + 3 * naw) // LANES):
        c0 = j * LANES
        x = p_ref[:, c0:c0 + LANES]
        if c0 < 2 * daw:
            partner = jnp.where(first_half, pltpu.roll(x, LANES - quarter, 1), pltpu.roll(x, quarter, 1))
            x = x * cos + partner * sin
            if c0 < daw:
                x = x * sa
        elif 3 * daw <= c0 < 3 * daw + naw:
            x = x * sn
        o_ref[:, c0:c0 + LANES] = x.astype(o_ref.dtype)


def _rope_tables(n_lat, n_ctx):
    t = jnp.arange(n_lat, dtype=jnp.int32)
    rows = (t // GRID_W).astype(F32)
    cols = (t % GRID_W).astype(F32)
    n_freq = DA_QK_DIM // 4
    inv_freq = ROPE_THETA ** (-jnp.arange(n_freq, dtype=F32) / n_freq)
    ang_r = rows[:, None] * inv_freq
    ang_c = cols[:, None] * inv_freq
    ang = jnp.concatenate([ang_r, ang_r, ang_c, ang_c], axis=1)
    sign = jnp.concatenate([-jnp.ones((n_freq,), F32), jnp.ones((n_freq,), F32)] * 2)
    cos = jnp.cos(ang)
    sin = jnp.sin(ang) * sign
    reps = LANES // DA_QK_DIM
    cos = jnp.concatenate([jnp.tile(cos, (1, reps)), jnp.ones((n_ctx, LANES), F32)], axis=0)
    sin = jnp.concatenate([jnp.tile(sin, (1, reps)), jnp.zeros((n_ctx, LANES), F32)], axis=0)
    return cos, sin


def _attn_prep(p, cos, sin, daw, naw):
    t, n = p.shape
    kern = functools.partial(_attn_prep_kernel, daw=daw, naw=naw, sa=DA_QK_DIM ** -0.5 * math.log2(math.e),
                             sn=LANES ** -0.5)
    return pl.pallas_call(
        kern,
        grid=(t // ROW_TILE,),
        in_specs=[_row_spec(n), _row_spec(LANES), _row_spec(LANES)],
        out_specs=_row_spec(n),
        out_shape=jax.ShapeDtypeStruct((t, n), BF16),
        compiler_params=_cp("arbitrary"),
        name="attn_prep",
    )(p, cos, sin)


def _diff_attn_kernel(lam_ref, q_ref, k_ref, v_ref, g_ref, o_ref, qm_ref, m_ref, acc_ref, *, nk, rg):
    kv = pl.program_id(2)
    tq = q_ref.shape[0]

    @pl.when(kv == 0)
    def _():
        q = q_ref[...]
        lane = lax.broadcasted_iota(jnp.int32, q.shape, 1)
        zero = jnp.zeros_like(q)
        qm_ref[0] = jnp.where(lane < DA_QK_DIM, q, zero)
        qm_ref[1] = jnp.where(lane >= DA_QK_DIM, q, zero)
        m_ref[...] = jnp.full(m_ref.shape, NEG, F32)
        acc_ref[...] = jnp.zeros(acc_ref.shape, F32)

    k = k_ref[...]
    v = v_ref[...]
    vext = jnp.concatenate([v, jnp.ones(v.shape, v.dtype)], axis=1)
    for r0 in range(0, tq, rg):
        for mi in range(2):
            s = lax.dot_general(qm_ref[mi, r0:r0 + rg], k, (((1,), (1,)), ((), ())), preferred_element_type=F32)
            m_prev = m_ref[mi, r0:r0 + rg]
            m_new = jnp.maximum(m_prev, jnp.max(s, axis=1, keepdims=True))
            p = jnp.exp2(s - m_new)
            acc_ref[mi, r0:r0 + rg] = (jnp.exp2(m_prev - m_new) * acc_ref[mi, r0:r0 + rg]
                                       + jnp.dot(p.astype(BF16), vext, preferred_element_type=F32))
            m_ref[mi, r0:r0 + rg] = m_new

    @pl.when(kv == nk - 1)
    def _():
        a0 = acc_ref[0]
        a1 = acc_ref[1]
        o = a0[:, :LANES] / a0[:, LANES:LANES + 1] - lam_ref[0] * (a1[:, :LANES] / a1[:, LANES:LANES + 1])
        o_ref[...] = (_rms(o, g_ref[...]) * lam_ref[1]).astype(o_ref.dtype)


DA_TQ, DA_TK, DA_RG = 1024, 2816, 256


def _diff_attn(qkv, lam2, subln_g, q_row0, n_q, k_row0, n_k, heads):
    tq = _pick(n_q, (DA_TQ, 1024, 512, 256))
    tk = _pick(n_k, (DA_TK, 1408, 1280, 1024, 768, 512, 256))
    assert q_row0 % tq == 0 and k_row0 % tk == 0
    nq, nk = n_q // tq, n_k // tk
    qb, kb = q_row0 // tq, k_row0 // tk
    return pl.pallas_call(
        functools.partial(_diff_attn_kernel, nk=nk, rg=min(DA_RG, tq)),
        grid=(heads, nq, nk),
        in_specs=[pl.BlockSpec(memory_space=pltpu.SMEM),
                  pl.BlockSpec((tq, LANES), lambda h, i, j: (qb + i, h)),
                  pl.BlockSpec((tk, LANES), lambda h, i, j: (kb + j, heads + h)),
                  pl.BlockSpec((tk, LANES), lambda h, i, j: (kb + j, 2 * heads + h)),
                  pl.BlockSpec((1, LANES), lambda h, i, j: (0, 0))],
        out_specs=pl.BlockSpec((tq, LANES), lambda h, i, j: (i, h)),
        out_shape=jax.ShapeDtypeStruct((n_q, heads * LANES), BF16),
        scratch_shapes=[pltpu.VMEM((2, tq, LANES), BF16), pltpu.VMEM((2, tq, 1), F32),
                        pltpu.VMEM((2, tq, 2 * LANES), F32)],
        compiler_params=_cp("arbitrary", "arbitrary", "arbitrary"),
        name="diff_attn",
    )(lam2, qkv, qkv, qkv, subln_g.reshape(1, LANES))


def _na_bias_tables(rpb, rows):
    wr, wc = NA_WIN_ROWS, NA_WIN_COLS
    heads = rpb.shape[0]
    c = np.arange(GRID_W)[:, None]
    kc = np.arange(GRID_W)[None, :]
    cs = np.clip(c - wc // 2, 0, GRID_W - wc)
    cvalid = (kc >= cs) & (kc < cs + wc)
    onehot = ((kc - c + wc - 1)[None] == np.arange(2 * wc - 1)[:, None, None]) & cvalid[None]
    sel = jnp.asarray(onehot.reshape(2 * wc - 1, GRID_W * GRID_W), F32)
    toe = jnp.dot(rpb.astype(F32).reshape(heads * (2 * wr - 1), 2 * wc - 1), sel,
                  precision=lax.Precision.HIGHEST)
    toe = toe.reshape(heads, 2 * wr - 1, GRID_W, GRID_W) + jnp.asarray(np.where(cvalid, 0.0, NEG), F32)
    toe = jnp.concatenate([toe, jnp.full((heads, 1, GRID_W, GRID_W), NEG, F32)], axis=1)
    r0s = np.array([0, NA_QROWS, rows - NA_QROWS])
    ks = np.clip(r0s - wr // 2, 0, rows - NA_KROWS)
    r = r0s[:, None, None] + np.arange(NA_QROWS)[None, :, None]
    kr = ks[:, None, None] + np.arange(NA_KROWS)[None, None, :]
    rs = np.clip(r - wr // 2, 0, rows - wr)
    idx = np.where((kr >= rs) & (kr < rs + wr), kr - r + wr - 1, 2 * wr - 1)
    blocks = toe[:, idx]
    return blocks.transpose(0, 1, 2, 4, 3, 5).reshape(heads, 3, NA_QROWS * GRID_W, NA_KROWS * GRID_W)


def _na_kernel(q_ref, k_ref, v_ref, b_ref, o_ref, *, n_lat, n_ctx, rows):
    rb = pl.program_id(1)
    n_rb = rows // NA_QROWS
    nt = (((1,), (1,)), ((), ()))
    heads = [slice(hh * LANES, (hh + 1) * LANES) for hh in range(NA_HPS)]

    def ctx_scores(c):
        kc = k_ref[n_lat:n_lat + n_ctx, c]
        return lax.dot_general(q_ref[:, c], kc, nt, preferred_element_type=F32), v_ref[n_lat:n_lat + n_ctx, c]

    @pl.when(rb < n_rb)
    def _():
        ks = jnp.clip(rb * NA_QROWS - NA_WIN_ROWS // 2, 0, rows - NA_KROWS)
        start = pl.multiple_of(ks * GRID_W, GRID_W)
        for hh, c in enumerate(heads):
            s_ctx, vc = ctx_scores(c)
            kw = k_ref[pl.ds(start, NA_KROWS * GRID_W), c]
            vw = v_ref[pl.ds(start, NA_KROWS * GRID_W), c]
            s_loc = lax.dot_general(q_ref[:, c], kw, nt, preferred_element_type=F32) + b_ref[hh, 0]
            m = jnp.maximum(jnp.max(s_loc, axis=1, keepdims=True), jnp.max(s_ctx, axis=1, keepdims=True))
            p_loc = jnp.exp(s_loc - m)
            p_ctx = jnp.exp(s_ctx - m)
            l = jnp.sum(p_loc, axis=1, keepdims=True) + jnp.sum(p_ctx, axis=1, keepdims=True)
            o = (jnp.dot(p_ctx.astype(BF16), vc, preferred_element_type=F32)
                 + jnp.dot(p_loc.astype(BF16), vw, preferred_element_type=F32))
            o_ref[:, c] = (o / l).astype(o_ref.dtype)

    @pl.when(rb == n_rb)
    def _():
        for c in heads:
            s_ctx, vc = ctx_scores(c)
            m = jnp.max(s_ctx, axis=1, keepdims=True)
            p = jnp.exp(s_ctx - m)
            l = jnp.sum(p, axis=1, keepdims=True)
            o = jnp.dot(p.astype(BF16), vc, preferred_element_type=F32)
            o_ref[:, c] = (o / l).astype(o_ref.dtype)


def _na_attn(qkv, bias, n_lat, n_ctx, da_heads, heads):
    t = qkv.shape[0]
    rows = n_lat // GRID_W
    n_rb = rows // NA_QROWS
    tq = NA_QROWS * GRID_W
    assert tq == ROW_TILE and n_ctx == ROW_TILE
    assert heads % NA_HPS == 0 and (3 * da_heads) % NA_HPS == 0
    hg = heads // NA_HPS
    w = NA_HPS * LANES
    q0, k0, v0 = 3 * da_heads // NA_HPS, 3 * da_heads // NA_HPS + hg, 3 * da_heads // NA_HPS + 2 * hg

    def bias_map(h, rb):
        return (h, jnp.where(rb == 0, 0, jnp.where(rb == n_rb - 1, 2, 1)), 0, 0)

    return pl.pallas_call(
        functools.partial(_na_kernel, n_lat=n_lat, n_ctx=n_ctx, rows=rows),
        grid=(hg, n_rb + 1),
        in_specs=[pl.BlockSpec((tq, w), lambda h, rb: (rb, q0 + h)),
                  pl.BlockSpec((t, w), lambda h, rb: (0, k0 + h)),
                  pl.BlockSpec((t, w), lambda h, rb: (0, v0 + h)),
                  pl.BlockSpec((NA_HPS, 1, tq, NA_KROWS * GRID_W), bias_map)],
        out_specs=pl.BlockSpec((tq, w), lambda h, rb: (rb, h)),
        out_shape=jax.ShapeDtypeStruct((t, heads * LANES), BF16),
        compiler_params=_cp("arbitrary", "arbitrary"),
        name="na_attn",
    )(qkv, qkv, qkv, bias)


def _hgrn_prep(q, v, z, lb, rev):
    c = HG_CHUNK
    f = lb + (1.0 - lb) * jax.nn.sigmoid(z)
    kk = 1.0 - f
    g = jnp.log2(f)
    ri = lax.broadcasted_iota(jnp.int32, (c, c), 0)
    ci = lax.broadcasted_iota(jnp.int32, (c, c), 1)
    tri = ((ri <= ci) if rev else (ri >= ci)).astype(BF16)
    g_hi = g.astype(BF16)
    r1 = g - g_hi.astype(F32)
    g_mid = r1.astype(BF16)
    g_lo = (r1 - g_mid.astype(F32)).astype(BF16)
    c2 = jnp.dot(tri, jnp.concatenate([g_hi, g_mid], axis=1), preferred_element_type=F32)
    cum = c2[:, :LANES] + c2[:, LANES:] + jnp.dot(tri, g_lo, preferred_element_type=F32)
    total = jnp.sum(g, axis=0, keepdims=True)
    return dict(cum=cum, total=total, kk=kk, qs=_silu(q) * (LANES ** -0.5), v=v, vb=v.astype(BF16))


def _hgrn_intra(pp, rev):
    c, sub, nsub = HG_CHUNK, HG_SUB, HG_CHUNK // HG_SUB
    nt = (((1,), (1,)), ((), ()))
    cum, kk, qs, v, vb = pp["cum"], pp["kk"], pp["qs"], pp["v"], pp["vb"]
    ngrp = sub // SUBLANES
    ti = lax.broadcasted_iota(jnp.int32, (SUBLANES, LANES), 0)
    ckey = cum - jnp.log2(kk)
    o_blocks = [None] * nsub
    for i in range(nsub):
        lo = (nsub - 1 - i) * sub if rev else i * sub
        hi = lo + sub
        cum_i, q_i = cum[lo:hi], qs[lo:hi]
        og = [None] * ngrp
        if i > 0:
            brow = hi if rev else lo - 1
            bnd = cum[brow:brow + 1]
            elo, ehi = (hi, c) if rev else (0, lo)
            qt = (q_i * jnp.exp2(cum_i - bnd)).astype(BF16)
            kt = (kk[elo:ehi] * jnp.exp2(bnd - cum[elo:ehi])).astype(BF16)
            att = lax.dot_general(qt, kt, nt, preferred_element_type=F32)
            o_i = jnp.dot(att.astype(BF16), vb[elo:ehi], preferred_element_type=F32)
            og = [o_i[gq * SUBLANES:(gq + 1) * SUBLANES] for gq in range(ngrp)]
        zs, dst = [], []
        for s in range(sub):
            gs = s // SUBLANES
            for gq in (range(gs + 1) if rev else range(gs, ngrp)):
                r = slice(gq * SUBLANES, (gq + 1) * SUBLANES)
                dlt = cum_i[r] - ckey[lo + s:lo + s + 1]
                if gq == gs:
                    dlt = jnp.where((ti <= s - r.start) if rev else (ti >= s - r.start), dlt, NEG)
                zs.append(jnp.exp2(dlt) * q_i[r])
                dst.append((gq, s))
        a_all = jnp.sum(jnp.concatenate(zs, axis=0), axis=1, keepdims=True)
        for n, (gq, s) in enumerate(dst):
            term = a_all[n * SUBLANES:(n + 1) * SUBLANES] * v[lo + s:lo + s + 1]
            og[gq] = term if og[gq] is None else og[gq] + term
        o_blocks[(nsub - 1 - i) if rev else i] = jnp.concatenate(og, axis=0)
    return jnp.concatenate(o_blocks, axis=0)


def _hgrn_carry(pp, st):
    cum, total = pp["cum"], pp["total"]
    o_inter = lax.dot_general((pp["qs"] * jnp.exp2(cum)).astype(BF16), st.astype(BF16),
                              (((1,), (1,)), ((), ())), preferred_element_type=F32)
    kdec = (pp["kk"] * jnp.exp2(total - cum)).astype(BF16)
    st_new = st * jnp.exp2(total) + lax.dot_general(pp["vb"], kdec, (((0,), (0,)), ((), ())),
                                                    preferred_element_type=F32)
    return o_inter, st_new


def _hgrn_kernel(*refs, rev, final, n_chunks):
    if final:
        q_ref, v_ref, z_ref, lb_ref, of_ref, gate_ref, gn_ref, o_ref, st_ref = refs
    else:
        q_ref, v_ref, z_ref, lb_ref, o_ref, st_ref = refs

    @pl.when(pl.program_id(1) == 0)
    def _():
        st_ref[...] = jnp.zeros(st_ref.shape, F32)

    order = [(n_chunks - 1 - s) if rev else s for s in range(n_chunks)]
    tiles = [(slice(ch * HG_CHUNK, (ch + 1) * HG_CHUNK), hh, slice(hh * LANES, (hh + 1) * LANES))
             for ch in order for hh in range(HG_HPS)]
    preps = [_hgrn_prep(q_ref[rows, cols], v_ref[rows, cols], z_ref[rows, cols], lb_ref[:, cols], rev)
             for rows, _, cols in tiles]
    intras = [_hgrn_intra(pp, rev) for pp in preps]
    states = [st_ref[hh] for hh in range(HG_HPS)]
    for (rows, hh, cols), pp, o in zip(tiles, preps, intras):
        o_inter, states[hh] = _hgrn_carry(pp, states[hh])
        o = o + o_inter
        if final:
            o = _rms(o + of_ref[rows, cols], gn_ref[...]) * _silu(gate_ref[rows, cols])
        o_ref[rows, cols] = o.astype(o_ref.dtype)
    for hh in range(HG_HPS):
        st_ref[hh] = states[hh]


def _hgrn_dir(p, lb_dir, rev, n_lat, heads, o_fwd=None, gnorm_g=None):
    t = p.shape[0]
    d = heads * LANES
    n_blk = t // ROW_TILE
    last = n_blk - 1
    final = o_fwd is not None

    def tok(j):
        return jnp.where(j == 0, last, (last - j) if rev else (j - 1))

    assert heads % HG_HPS == 0
    hg = heads // HG_HPS
    w = HG_HPS * LANES
    zcol = 3 * hg if rev else 2 * hg
    in_specs = [pl.BlockSpec((ROW_TILE, w), lambda h, j: (tok(j), h)),
                pl.BlockSpec((ROW_TILE, w), lambda h, j: (tok(j), hg + h)),
                pl.BlockSpec((ROW_TILE, w), lambda h, j: (tok(j), zcol + h)),
                pl.BlockSpec((1, w), lambda h, j: (0, h))]
    args = [p, p, p, lb_dir.reshape(1, d)]
    if final:
        in_specs += [pl.BlockSpec((ROW_TILE, w), lambda h, j: (tok(j), h)),
                     pl.BlockSpec((ROW_TILE, w), lambda h, j: (tok(j), 4 * hg + h)),
                     pl.BlockSpec((1, LANES), lambda h, j: (0, 0))]
        args += [o_fwd, p, gnorm_g.reshape(1, LANES)]
    return pl.pallas_call(
        functools.partial(_hgrn_kernel, rev=rev, final=final, n_chunks=ROW_TILE // HG_CHUNK),
        grid=(hg, n_blk),
        in_specs=in_specs,
        out_specs=pl.BlockSpec((ROW_TILE, w), lambda h, j: (tok(j), h)),
        out_shape=jax.ShapeDtypeStruct((t, d), BF16 if final else F32),
        scratch_shapes=[pltpu.VMEM((HG_HPS, LANES, LANES), F32)],
        compiler_params=_cp("arbitrary", "arbitrary"),
        name="hgrn_bwd" if rev else "hgrn_fwd",
    )(*args)


def _new_expert(be_ref):
    b = pl.program_id(0)
    return jnp.logical_or(b == 0, be_ref[b] != be_ref[jnp.maximum(b - 1, 0)])


def _expert_weights(be_ref, nu_ref, ws_ref, nx_ref, w_hbms, wbuf, wsem, wb_refs, layer):
    b = pl.program_id(0)

    def copies(expert, slot):
        return [pltpu.make_async_copy(w.at[layer, expert], wbuf.at[slot, i], wsem.at[slot])
                for i, w in enumerate(w_hbms)]

    @pl.when(b == 0)
    def _():
        for cp in copies(be_ref[0], ws_ref[0]):
            cp.start()

    @pl.when(jnp.logical_and(b < nu_ref[0], _new_expert(be_ref)))
    def _():
        slot = ws_ref[b]
        for cp in copies(be_ref[b], slot):
            cp.wait()

        @pl.when(nx_ref[b] >= 0)
        def _():
            for cp in copies(nx_ref[b], 1 - slot):
                cp.start()

        for i, wb in enumerate(wb_refs):
            wb[...] = wbuf[slot, i].astype(BF16)


def _moe_up_kernel(be_ref, nu_ref, ws_ref, nx_ref, base_ref, nval_ref, tok_ref, x_hbm, wg_hbm, wu_hbm, o_ref,
                   wgb_ref, wub_ref, xbuf, sem, wbuf, wsem, *, layer):
    b = pl.program_id(0)
    nch = wgb_ref.shape[0] // LANES

    def gather(blk, start):
        base = base_ref[blk]
        last = jnp.maximum(nval_ref[blk] - 1, 0)
        _row_gather(lambda r: tok_ref[base + jnp.minimum(r, last)], x_hbm, xbuf.at[blk % 2], sem.at[blk % 2],
                    MOE_BM, start, span=nch)

    @pl.when(b == 0)
    def _():
        gather(b, True)

    @pl.when(b + 1 < nu_ref[0])
    def _():
        gather(b + 1, True)

    _expert_weights(be_ref, nu_ref, ws_ref, nx_ref, (wg_hbm, wu_hbm), wbuf, wsem, (wgb_ref, wub_ref), layer)

    @pl.when(b < nu_ref[0])
    def _():
        gather(b, False)
        xb = xbuf.at[b % 2]
        x = jnp.concatenate([xb[pl.ds(j, MOE_BM, stride=nch), :] for j in range(nch)], axis=1).astype(BF16)
        hg = jnp.dot(x, wgb_ref[...], preferred_element_type=F32)
        hu = jnp.dot(x, wub_ref[...], preferred_element_type=F32)
        o_ref[...] = (_silu(hg) * hu).astype(o_ref.dtype)

    @pl.when(b >= nu_ref[0])
    def _():
        o_ref[...] = jnp.zeros(o_ref.shape, o_ref.dtype)


def _moe_down_kernel(be_ref, nu_ref, ws_ref, nx_ref, h_ref, wd_hbm, o_ref, wdb_ref, wbuf, wsem, *, layer):
    b = pl.program_id(0)
    _expert_weights(be_ref, nu_ref, ws_ref, nx_ref, (wd_hbm,), wbuf, wsem, (wdb_ref,), layer)

    @pl.when(b < nu_ref[0])
    def _():
        o_ref[...] = jnp.dot(h_ref[...], wdb_ref[...], preferred_element_type=F32)

    @pl.when(b >= nu_ref[0])
    def _():
        o_ref[...] = jnp.zeros(o_ref.shape, o_ref.dtype)


def _moe_experts(m, tok_sorted, blk_expert, blk_base, blk_nval, n_used, blk_wslot, blk_next, w_gate, w_up,
                 w_down, layer):
    d = w_gate.shape[2]
    n_blk = blk_expert.shape[0]
    n_slot = n_blk * MOE_BM
    de = w_gate.shape[3]
    hbm = pl.BlockSpec(memory_space=pl.ANY)
    hid = pl.pallas_call(
        functools.partial(_moe_up_kernel, layer=layer),
        grid_spec=pltpu.PrefetchScalarGridSpec(
            num_scalar_prefetch=7, grid=(n_blk,),
            in_specs=[hbm, hbm, hbm],
            out_specs=pl.BlockSpec((MOE_BM, de), lambda b, *_: (b, 0)),
            scratch_shapes=[pltpu.VMEM((d, de), BF16), pltpu.VMEM((d, de), BF16),
                            pltpu.VMEM((2, MOE_BM * (d // LANES), LANES), F32), pltpu.SemaphoreType.DMA((2,)),
                            pltpu.VMEM((2, 2, d, de), F32), pltpu.SemaphoreType.DMA((2,))]),
        out_shape=jax.ShapeDtypeStruct((n_slot, de), BF16),
        compiler_params=_cp("arbitrary"),
        name="moe_up",
    )(blk_expert, n_used, blk_wslot, blk_next, blk_base, blk_nval, tok_sorted, m, w_gate, w_up)
    return pl.pallas_call(
        functools.partial(_moe_down_kernel, layer=layer),
        grid_spec=pltpu.PrefetchScalarGridSpec(
            num_scalar_prefetch=4, grid=(n_blk,),
            in_specs=[pl.BlockSpec((MOE_BM, de), lambda b, *_: (b, 0)), hbm],
            out_specs=pl.BlockSpec((MOE_BM, d), lambda b, *_: (b, 0)),
            scratch_shapes=[pltpu.VMEM((de, d), BF16), pltpu.VMEM((2, 1, de, d), F32),
                            pltpu.SemaphoreType.DMA((2,))]),
        out_shape=jax.ShapeDtypeStruct((n_slot, d), F32),
        compiler_params=_cp("arbitrary"),
        name="moe_down",
    )(blk_expert, n_used, blk_wslot, blk_next, hid, w_down)


def _route(logits):
    n_exp = N_GROUPS * EXPERTS_PER_GROUP
    t = logits.shape[0]
    g_logits = logits[:, :N_GROUPS]
    g_prob = jax.nn.softmax(g_logits, axis=-1)
    g_sel = jnp.argmax(g_logits, axis=-1).astype(jnp.int32)
    g_w = jnp.take_along_axis(g_prob, g_sel[:, None], axis=1)[:, 0]
    e_logits = logits[:, N_GROUPS:N_GROUPS + n_exp].reshape(t, N_GROUPS, EXPERTS_PER_GROUP)
    e_logits = jnp.take_along_axis(e_logits, g_sel[:, None, None], axis=1)[:, 0]
    top_v, top_i = lax.top_k(e_logits, TOP_K)
    weights = g_w[:, None] * jax.nn.softmax(top_v, axis=-1)
    expert = g_sel[:, None] * EXPERTS_PER_GROUP + top_i.astype(jnp.int32)
    return expert, weights


def _dispatch(expert, weights):
    n_exp = N_GROUPS * EXPERTS_PER_GROUP
    t = expert.shape[0]
    n_as = t * TOP_K
    eid = expert.reshape(-1)
    aidx = jnp.arange(n_as, dtype=jnp.int32)
    eid_s, order = lax.sort((eid, aidx), num_keys=1, is_stable=True)
    onehot = eid_s[:, None] == jnp.arange(n_exp, dtype=jnp.int32)[None, :]
    counts = jnp.sum(onehot, axis=0, dtype=jnp.int32)
    padded = (counts + MOE_BM - 1) // MOE_BM * MOE_BM
    start = jnp.cumsum(counts) - counts
    pend = jnp.cumsum(padded)
    pstart = pend - padded
    dest = aidx + jnp.sum(jnp.where(onehot, (pstart - start)[None, :], 0), axis=1)
    n_blk = -(-n_as // MOE_BM) + n_exp
    blk_expert = jnp.searchsorted(pend, jnp.arange(n_blk, dtype=jnp.int32) * MOE_BM, side='right')
    blk_expert = jnp.minimum(blk_expert, n_exp - 1).astype(jnp.int32)
    n_used = (pend[-1] // MOE_BM).astype(jnp.int32).reshape(1)
    blk_first = jnp.arange(n_blk, dtype=jnp.int32) * MOE_BM - pstart[blk_expert]
    blk_base = jnp.clip(start[blk_expert] + blk_first, 0, n_as - 1).astype(jnp.int32)
    blk_nval = jnp.clip(counts[blk_expert] - blk_first, 0, MOE_BM).astype(jnp.int32)
    blk_base = jnp.minimum(blk_base, n_as - jnp.maximum(blk_nval, 1))
    _, slot_of = lax.sort((order, dest), num_keys=1)
    slot_idx = slot_of.reshape(t, TOP_K).T.reshape(-1)
    used = counts > 0
    ids = jnp.arange(n_exp, dtype=jnp.int32)
    wslot_e = (jnp.cumsum(used.astype(jnp.int32)) - 1) % 2
    later = jnp.where(used[None, :] & (ids[None, :] > ids[:, None]), ids[None, :], n_exp)
    next_e = jnp.min(later, axis=1)
    next_e = jnp.where(next_e == n_exp, -1, next_e).astype(jnp.int32)
    blk_wslot = jnp.maximum(wslot_e, 0)[blk_expert].astype(jnp.int32)
    blk_next = next_e[blk_expert]
    return order // TOP_K, blk_expert, blk_base, blk_nval, n_used, blk_wslot, blk_next, slot_idx


def _moe(m, logits, w_gate, w_up, w_down, layer):
    expert, weights = _route(logits)
    tok_sorted, blk_expert, blk_base, blk_nval, n_used, blk_wslot, blk_next, slot_idx = _dispatch(expert, weights)
    ys = _moe_experts(m, tok_sorted, blk_expert, blk_base, blk_nval, n_used, blk_wslot, blk_next,
                      w_gate, w_up, w_down, layer)
    return ys, slot_idx, weights.T[:, :, None]


def _diff_lambda(lam_p, layer_idx):
    lam_init = 0.8 - 0.6 * math.exp(-0.3 * layer_idx)
    lf = lam_p.astype(F32)
    lam = jnp.exp(jnp.sum(lf[0] * lf[1])) - jnp.exp(jnp.sum(lf[2] * lf[3])) + lam_init
    return jnp.stack([lam, jnp.asarray(1.0 - lam_init, F32)]).astype(F32)


def kernel(x, c, ctx, c_ctx, w_mod, b_mod, norm1_g, norm2_g, att_w_in, att_w_out, att_lambda, att_subln_g,
           att_rpb, rec_w_in, rec_w_out, rec_lb_logits, rec_gnorm_g, moe_w_group, moe_b_group, moe_w_router,
           moe_b_router, moe_w_gate, moe_w_up, moe_w_down, final_norm_g):
    b_, n_lat, d = x.shape
    n_ctx = ctx.shape[1]
    assert b_ == 1 and n_ctx == ROW_TILE and n_lat % ROW_TILE == 0 and d % LANES == 0
    depth = w_mod.shape[0]
    daw, naw = DA_HEADS * LANES, NA_HEADS * LANES
    n_exp = N_GROUPS * EXPERTS_PER_GROUP

    lbp = jax.nn.softmax(rec_lb_logits.astype(F32), axis=0)
    lbs = jnp.cumsum(lbp, axis=0) - lbp[0:1]

    cvec = jnp.zeros((8, d), F32).at[0].set(c[0]).at[1].set(c_ctx)
    mods = _modulation(cvec, w_mod, b_mod)
    cos, sin = _rope_tables(n_lat, n_ctx)

    h = jnp.concatenate([x[0], ctx[0]], axis=0)
    a = _ln_mod(h, norm1_g[0], mods[0], 0, 1, n_lat)
    out = None
    for l in range(depth):
        j = l // 2
        mod = mods[l]
        if l % 2 == 0:
            p = _matmul(a, att_w_in, j, F32)
            qkv = _attn_prep(p, cos, sin, daw, naw)
            lam2 = _diff_lambda(att_lambda[j], l)
            da_lat = _diff_attn(qkv, lam2, att_subln_g[j], 0, n_lat, 0, n_lat + n_ctx, DA_HEADS)
            da_ctx = _diff_attn(qkv, lam2, att_subln_g[j], n_lat, n_ctx, n_lat, n_ctx, DA_HEADS)
            bias = _na_bias_tables(att_rpb[j], n_lat // GRID_W)
            na = _na_attn(qkv, bias, n_lat, n_ctx, DA_HEADS, NA_HEADS)
            cat = jnp.concatenate([jnp.concatenate([da_lat, da_ctx], axis=0), na], axis=1)
            y = _matmul(cat, att_w_out, j, F32)
        else:
            p = _matmul(a, rec_w_in, j, F32)
            o_f = _hgrn_dir(p, lbs[l, 0], False, n_lat, HG_HEADS)
            o = _hgrn_dir(p, lbs[l, 1], True, n_lat, HG_HEADS, o_fwd=o_f, gnorm_g=rec_gnorm_g[j])
            y = _matmul(o, rec_w_out, j, F32)
        w_route = jnp.zeros((d, LANES), F32).at[:, :N_GROUPS].set(moe_w_group[l])
        w_route = w_route.at[:, N_GROUPS:N_GROUPS + n_exp].set(moe_w_router[l])
        b_route = jnp.zeros((1, LANES), F32).at[0, :N_GROUPS].set(moe_b_group[l])
        b_route = b_route.at[0, N_GROUPS:N_GROUPS + n_exp].set(moe_b_router[l])
        h, m, logits = _res_ln_mod_route(h, y, norm2_g[l], mod, 2, 3, 4, w_route, b_route, n_lat)
        ys, slot_idx, wt = _moe(m, logits, moe_w_gate, moe_w_up, moe_w_down, l)
        if l + 1 < depth:
            h, a = _res_ln_mod(h, ys, slot_idx, wt, norm1_g[l + 1], mod, 5, mods[l + 1], 0, 1, n_lat)
        else:
            out = _res_final(h, ys, slot_idx, wt, final_norm_g, mod, 5, n_lat)
    return out[None]
```

```python
import functools
import math

import numpy as np
import jax
import jax.numpy as jnp
from jax import lax
from jax.experimental import pallas as pl
from jax.experimental.pallas import tpu as pltpu

F32 = jnp.float32
BF16 = jnp.bfloat16

GRID_W = 64
EPS = 1e-6
DA_HEADS = 8
DA_QK_DIM = 64
NA_HEADS = 8
NA_WIN_ROWS = 8
NA_WIN_COLS = 16
ROPE_THETA = 10000.0
HG_HEADS = 16
N_GROUPS = 4
EXPERTS_PER_GROUP = 8
TOP_K = 2

LANES = 128
SUBLANES = 8
ROW_TILE = 256
HG_CHUNK = 64
HG_SUB = 32
HG_HPS = 2
NA_QROWS = 4
NA_KROWS = 12
NA_HPS = 4
MOE_BM = 256
NEG = -1e30
VMEM_LIMIT = 56 << 20


def _cp(*sem):
    return pltpu.CompilerParams(dimension_semantics=sem, vmem_limit_bytes=VMEM_LIMIT)


def _pick(n, cands):
    for c in cands:
        if n % c == 0:
            return c
    raise ValueError(f"no tile for {n} in {cands}")


def _silu(x):
    return x * jax.nn.sigmoid(x)


def _rms(x, g):
    return x * lax.rsqrt(jnp.mean(x * x, axis=-1, keepdims=True) + EPS) * g


def _mod_kernel(c_ref, w_ref, b_ref, o_ref):
    s = _silu(c_ref[...]).astype(BF16)
    o_ref[0] = jnp.dot(s, w_ref[0].astype(BF16), preferred_element_type=F32) + b_ref[0]


def _modulation(cvec, w_mod, b_mod):
    depth, d, n = w_mod.shape
    tn = _pick(n, (1024, 512, 256, 128))
    return pl.pallas_call(
        _mod_kernel,
        grid=(depth, n // tn),
        in_specs=[pl.BlockSpec((8, d), lambda l, j: (0, 0)),
                  pl.BlockSpec((1, d, tn), lambda l, j: (l, 0, j)),
                  pl.BlockSpec((1, 1, tn), lambda l, j: (l, 0, j))],
        out_specs=pl.BlockSpec((1, 8, tn), lambda l, j: (l, 0, j)),
        out_shape=jax.ShapeDtypeStruct((depth, 8, n), F32),
        compiler_params=_cp("arbitrary", "arbitrary"),
        name="modulation",
    )(cvec, w_mod, b_mod.reshape(depth, 1, n))


def _mod_row(ref, n_lat_tiles):
    r = (pl.program_id(0) >= n_lat_tiles).astype(jnp.int32)
    return ref[pl.ds(r, 1), :]


def _ln_mod_kernel(h_ref, g_ref, sh_ref, sc_ref, a_ref, *, n_lat_tiles):
    y = _rms(h_ref[...], g_ref[...])
    a = y * (1.0 + _mod_row(sc_ref, n_lat_tiles)) + _mod_row(sh_ref, n_lat_tiles)
    a_ref[...] = a.astype(a_ref.dtype)


def _row_gather(row_of, src_hbm, dst, sem, n_rows, start, span=1):
    def body(r, carry):
        cp = pltpu.make_async_copy(src_hbm.at[pl.ds(pl.multiple_of(row_of(r) * span, span), span)],
                                   dst.at[pl.ds(pl.multiple_of(r * span, span), span)], sem)
        if start:
            cp.start()
        else:
            cp.wait()
        return carry

    lax.fori_loop(0, n_rows, body, 0, unroll=8)


def _combined_expert_rows(idx_ref, wt_ref, ys_hbm, ybuf, sem, t):
    i = pl.program_id(0)

    def gather(tile, start):
        slot = tile % 2
        for k in range(TOP_K):
            base = k * t + tile * ROW_TILE
            _row_gather(lambda r: idx_ref[base + r], ys_hbm, ybuf.at[slot, k], sem.at[slot], ROW_TILE, start)

    @pl.when(i == 0)
    def _():
        gather(i, True)

    @pl.when(i + 1 < pl.num_programs(0))
    def _():
        gather(i + 1, True)

    gather(i, False)
    y = wt_ref[0] * ybuf[i % 2, 0]
    for k in range(1, TOP_K):
        y = y + wt_ref[k] * ybuf[i % 2, k]
    return y


def _res_ln_mod_kernel(idx_ref, h_ref, wt_ref, ys_hbm, gt_ref, g_ref, sh_ref, sc_ref, ho_ref, a_ref, ybuf, sem,
                       *, n_lat_tiles, t):
    y = _combined_expert_rows(idx_ref, wt_ref, ys_hbm, ybuf, sem, t)
    h = h_ref[...] + _mod_row(gt_ref, n_lat_tiles) * y
    ho_ref[...] = h
    a = _rms(h, g_ref[...]) * (1.0 + _mod_row(sc_ref, n_lat_tiles)) + _mod_row(sh_ref, n_lat_tiles)
    a_ref[...] = a.astype(a_ref.dtype)


def _res_ln_mod_route_kernel(h_ref, y_ref, gt_ref, g_ref, sh_ref, sc_ref, wr_ref, br_ref,
                             ho_ref, a_ref, lg_ref, *, n_lat_tiles):
    h = h_ref[...] + _mod_row(gt_ref, n_lat_tiles) * y_ref[...]
    ho_ref[...] = h
    a = _rms(h, g_ref[...]) * (1.0 + _mod_row(sc_ref, n_lat_tiles)) + _mod_row(sh_ref, n_lat_tiles)
    nch = a.shape[1] // LANES
    for j in range(nch):
        a_ref[pl.ds(j, ROW_TILE, stride=nch), :] = a[:, j * LANES:(j + 1) * LANES]
    lg_ref[...] = jnp.dot(a, wr_ref[...], preferred_element_type=F32,
                          precision=lax.Precision.HIGHEST) + br_ref[...]


def _res_final_kernel(idx_ref, h_ref, wt_ref, ys_hbm, gt_ref, g_ref, o_ref, ybuf, sem, *, t):
    y = _combined_expert_rows(idx_ref, wt_ref, ys_hbm, ybuf, sem, t)
    h = h_ref[...] + gt_ref[pl.ds(0, 1), :] * y
    o_ref[...] = _rms(h, g_ref[...])


def _row_spec(d):
    return pl.BlockSpec((ROW_TILE, d), lambda i, *_: (i, 0))


def _vec_spec(d):
    return pl.BlockSpec((1, d), lambda i, *_: (0, 0))


def _mod_spec(d, k):
    return pl.BlockSpec((8, d), lambda i, *_: (0, k))


def _combine_scratch(d):
    return [pltpu.VMEM((2, TOP_K, ROW_TILE, d), F32), pltpu.SemaphoreType.DMA((2,))]


def _wt_spec():
    return pl.BlockSpec((TOP_K, ROW_TILE, 1), lambda i, *_: (0, i, 0))


def _ln_mod(h, g, mod, k_shift, k_scale, n_lat):
    t, d = h.shape
    return pl.pallas_call(
        functools.partial(_ln_mod_kernel, n_lat_tiles=n_lat // ROW_TILE),
        grid=(t // ROW_TILE,),
        in_specs=[_row_spec(d), _vec_spec(d), _mod_spec(d, k_shift), _mod_spec(d, k_scale)],
        out_specs=_row_spec(d),
        out_shape=jax.ShapeDtypeStruct((t, d), BF16),
        compiler_params=_cp("arbitrary"),
        name="ln_mod",
    )(h, g.reshape(1, d), mod, mod)


def _res_ln_mod(h, ys, slot_idx, wt, g, mod_gate, k_gate, mod_next, k_shift, k_scale, n_lat):
    t, d = h.shape
    return pl.pallas_call(
        functools.partial(_res_ln_mod_kernel, n_lat_tiles=n_lat // ROW_TILE, t=t),
        grid_spec=pltpu.PrefetchScalarGridSpec(
            num_scalar_prefetch=1, grid=(t // ROW_TILE,),
            in_specs=[_row_spec(d), _wt_spec(), pl.BlockSpec(memory_space=pl.ANY), _mod_spec(d, k_gate),
                      _vec_spec(d), _mod_spec(d, k_shift), _mod_spec(d, k_scale)],
            out_specs=[_row_spec(d), _row_spec(d)],
            scratch_shapes=_combine_scratch(d)),
        out_shape=[jax.ShapeDtypeStruct((t, d), F32), jax.ShapeDtypeStruct((t, d), BF16)],
        compiler_params=_cp("arbitrary"),
        name="res_ln_mod",
    )(slot_idx, h, wt, ys, mod_gate, g.reshape(1, d), mod_next, mod_next)


def _res_ln_mod_route(h, y, g, mod, k_gate, k_shift, k_scale, w_route, b_route, n_lat):
    t, d = h.shape
    return pl.pallas_call(
        functools.partial(_res_ln_mod_route_kernel, n_lat_tiles=n_lat // ROW_TILE),
        grid=(t // ROW_TILE,),
        in_specs=[_row_spec(d), _row_spec(d), _mod_spec(d, k_gate), _vec_spec(d),
                  _mod_spec(d, k_shift), _mod_spec(d, k_scale),
                  pl.BlockSpec((d, LANES), lambda i: (0, 0)), _vec_spec(LANES)],
        out_specs=[_row_spec(d), pl.BlockSpec((ROW_TILE * (d // LANES), LANES), lambda i: (i, 0)),
                   _row_spec(LANES)],
        out_shape=[jax.ShapeDtypeStruct((t, d), F32), jax.ShapeDtypeStruct((t * (d // LANES), LANES), F32),
                   jax.ShapeDtypeStruct((t, LANES), F32)],
        compiler_params=_cp("arbitrary"),
        name="res_ln_mod_route",
    )(h, y, mod, g.reshape(1, d), mod, mod, w_route, b_route)


def _res_final(h, ys, slot_idx, wt, g, mod, k_gate, n_lat):
    t, d = h.shape
    return pl.pallas_call(
        functools.partial(_res_final_kernel, t=t),
        grid_spec=pltpu.PrefetchScalarGridSpec(
            num_scalar_prefetch=1, grid=(n_lat // ROW_TILE,),
            in_specs=[_row_spec(d), _wt_spec(), pl.BlockSpec(memory_space=pl.ANY), _mod_spec(d, k_gate),
                      _vec_spec(d)],
            out_specs=_row_spec(d),
            scratch_shapes=_combine_scratch(d)),
        out_shape=jax.ShapeDtypeStruct((n_lat, d), F32),
        compiler_params=_cp("arbitrary"),
        name="res_final",
    )(slot_idx, h, wt, ys, mod, g.reshape(1, d))


def _matmul_kernel(a_ref, w_ref, o_ref, wb_ref):
    @pl.when(pl.program_id(1) == 0)
    def _():
        wb_ref[...] = w_ref[0].astype(BF16)

    o_ref[...] = jnp.dot(a_ref[...], wb_ref[...], preferred_element_type=F32).astype(o_ref.dtype)


def _matmul2_kernel(a1_ref, a2_ref, w_ref, o_ref, wb_ref):
    @pl.when(pl.program_id(1) == 0)
    def _():
        wb_ref[...] = w_ref[0].astype(BF16)

    k1 = a1_ref.shape[1]
    o_ref[...] = (jnp.dot(a1_ref[...], wb_ref[:k1, :], preferred_element_type=F32)
                  + jnp.dot(a2_ref[...], wb_ref[k1:, :], preferred_element_type=F32)).astype(o_ref.dtype)


def _matmul(a, w_stack, layer, out_dtype, a2=None):
    m = a.shape[0]
    ops = [a] if a2 is None else [a, a2]
    k = w_stack.shape[1]
    n = w_stack.shape[2]
    assert sum(x.shape[1] for x in ops) == k
    tm = _pick(m, (768, 640, 512, 256))
    tn = _pick(n, (1024, 512, 256, 128))
    return pl.pallas_call(
        _matmul_kernel if a2 is None else _matmul2_kernel,
        grid=(n // tn, m // tm),
        in_specs=[pl.BlockSpec((tm, x.shape[1]), lambda j, i: (i, 0)) for x in ops]
        + [pl.BlockSpec((1, k, tn), lambda j, i: (layer, 0, j))],
        out_specs=pl.BlockSpec((tm, tn), lambda j, i: (i, j)),
        out_shape=jax.ShapeDtypeStruct((m, n), out_dtype),
        scratch_shapes=[pltpu.VMEM((k, tn), BF16)],
        compiler_params=_cp("arbitrary", "arbitrary"),
        name="matmul",
    )(*ops, w_stack)


def _attn_in_kernel(a_ref, w_ref, cos_ref, sin_ref, o_ref, wb_ref, *, daw, naw, sa, sn):
    @pl.when(pl.program_id(1) == 0)
    def _():
        wb_ref[...] = w_ref[0].astype(BF16)

    tn = o_ref.shape[1]
    c0 = pl.program_id(0) * tn
    p = jnp.dot(a_ref[...], wb_ref[...], preferred_element_type=F32)

    @pl.when(c0 < 2 * daw)
    def _():
        cos = cos_ref[...]
        sin = sin_ref[...]
        lane = lax.broadcasted_iota(jnp.int32, cos.shape, 1)
        first_half = (lane % (DA_QK_DIM // 2)) < (DA_QK_DIM // 4)
        quarter = DA_QK_DIM // 4
        scale = jnp.where(c0 < daw, sa, 1.0).astype(F32)
        for g in range(tn // LANES):
            x = p[:, g * LANES:(g + 1) * LANES]
            partner = jnp.where(first_half, pltpu.roll(x, LANES - quarter, 1), pltpu.roll(x, quarter, 1))
            o_ref[:, g * LANES:(g + 1) * LANES] = ((x * cos + partner * sin) * scale).astype(o_ref.dtype)

    @pl.when(c0 >= 2 * daw)
    def _():
        scale = jnp.where(jnp.logical_and(c0 >= 3 * daw, c0 < 3 * daw + naw), sn, 1.0).astype(F32)
        o_ref[...] = (p * scale).astype(o_ref.dtype)


def _rope_tables(n_lat, n_ctx):
    t = jnp.arange(n_lat, dtype=jnp.int32)
    rows = (t // GRID_W).astype(F32)
    cols = (t % GRID_W).astype(F32)
    n_freq = DA_QK_DIM // 4
    inv_freq = ROPE_THETA ** (-jnp.arange(n_freq, dtype=F32) / n_freq)
    ang_r = rows[:, None] * inv_freq
    ang_c = cols[:, None] * inv_freq
    ang = jnp.concatenate([ang_r, ang_r, ang_c, ang_c], axis=1)
    sign = jnp.concatenate([-jnp.ones((n_freq,), F32), jnp.ones((n_freq,), F32)] * 2)
    cos = jnp.cos(ang)
    sin = jnp.sin(ang) * sign
    reps = LANES // DA_QK_DIM
    cos = jnp.concatenate([jnp.tile(cos, (1, reps)), jnp.ones((n_ctx, LANES), F32)], axis=0)
    sin = jnp.concatenate([jnp.tile(sin, (1, reps)), jnp.zeros((n_ctx, LANES), F32)], axis=0)
    return cos, sin


def _attn_in_proj(a, w_stack, layer, cos, sin, daw, naw):
    m, k = a.shape
    n = w_stack.shape[2]
    tm = _pick(m, (768, 640, 512, 256))
    tn = _pick(math.gcd(daw, naw), (1024, 512, 256, 128))
    kern = functools.partial(_attn_in_kernel, daw=daw, naw=naw, sa=DA_QK_DIM ** -0.5 * math.log2(math.e),
                             sn=LANES ** -0.5)
    return pl.pallas_call(
        kern,
        grid=(n // tn, m // tm),
        in_specs=[pl.BlockSpec((tm, k), lambda j, i: (i, 0)),
                  pl.BlockSpec((1, k, tn), lambda j, i: (layer, 0, j)),
                  pl.BlockSpec((tm, LANES), lambda j, i: (i, 0)),
                  pl.BlockSpec((tm, LANES), lambda j, i: (i, 0))],
        out_specs=pl.BlockSpec((tm, tn), lambda j, i: (i, j)),
        out_shape=jax.ShapeDtypeStruct((m, n), BF16),
        scratch_shapes=[pltpu.VMEM((k, tn), BF16)],
        compiler_params=_cp("arbitrary", "arbitrary"),
        name="attn_in_proj",
    )(a, w_stack, cos, sin)


def _diff_attn_kernel(lam_ref, q_ref, k_ref, v_ref, g_ref, o_ref, qm_ref, m_ref, acc_ref, *, nk, rg):
    kv = pl.program_id(2)
    tq = q_ref.shape[0]

    @pl.when(kv == 0)
    def _():
        q = q_ref[...]
        lane = lax.broadcasted_iota(jnp.int32, q.shape, 1)
        zero = jnp.zeros_like(q)
        qm_ref[0] = jnp.where(lane < DA_QK_DIM, q, zero)
        qm_ref[1] = jnp.where(lane >= DA_QK_DIM, q, zero)
        m_ref[...] = jnp.full(m_ref.shape, NEG, F32)
        acc_ref[...] = jnp.zeros(acc_ref.shape, F32)

    k = k_ref[...]
    v = v_ref[...]
    vext = jnp.concatenate([v, jnp.ones(v.shape, v.dtype)], axis=1)
    for r0 in range(0, tq, rg):
        for mi in range(2):
            s = lax.dot_general(qm_ref[mi, r0:r0 + rg], k, (((1,), (1,)), ((), ())), preferred_element_type=F32)
            m_prev = m_ref[mi, r0:r0 + rg]
            m_new = jnp.maximum(m_prev, jnp.max(s, axis=1, keepdims=True))
            p = jnp.exp2(s - m_new)
            acc_ref[mi, r0:r0 + rg] = (jnp.exp2(m_prev - m_new) * acc_ref[mi, r0:r0 + rg]
                                       + jnp.dot(p.astype(BF16), vext, preferred_element_type=F32))
            m_ref[mi, r0:r0 + rg] = m_new

    @pl.when(kv == nk - 1)
    def _():
        a0 = acc_ref[0]
        a1 = acc_ref[1]
        o = a0[:, :LANES] / a0[:, LANES:LANES + 1] - lam_ref[0] * (a1[:, :LANES] / a1[:, LANES:LANES + 1])
        o_ref[...] = (_rms(o, g_ref[...]) * lam_ref[1]).astype(o_ref.dtype)


DA_TQ, DA_TK, DA_RG = 1024, 2816, 256


def _diff_attn(qkv, lam2, subln_g, q_row0, n_q, k_row0, n_k, heads):
    tq = _pick(n_q, (DA_TQ, 1024, 512, 256))
    tk = _pick(n_k, (DA_TK, 1408, 1280, 1024, 768, 512, 256))
    assert q_row0 % tq == 0 and k_row0 % tk == 0
    nq, nk = n_q // tq, n_k // tk
    qb, kb = q_row0 // tq, k_row0 // tk
    return pl.pallas_call(
        functools.partial(_diff_attn_kernel, nk=nk, rg=min(DA_RG, tq)),
        grid=(heads, nq, nk),
        in_specs=[pl.BlockSpec(memory_space=pltpu.SMEM),
                  pl.BlockSpec((tq, LANES), lambda h, i, j: (qb + i, h)),
                  pl.BlockSpec((tk, LANES), lambda h, i, j: (kb + j, heads + h)),
                  pl.BlockSpec((tk, LANES), lambda h, i, j: (kb + j, 2 * heads + h)),
                  pl.BlockSpec((1, LANES), lambda h, i, j: (0, 0))],
        out_specs=pl.BlockSpec((tq, LANES), lambda h, i, j: (i, h)),
        out_shape=jax.ShapeDtypeStruct((n_q, heads * LANES), BF16),
        scratch_shapes=[pltpu.VMEM((2, tq, LANES), BF16), pltpu.VMEM((2, tq, 1), F32),
                        pltpu.VMEM((2, tq, 2 * LANES), F32)],
        compiler_params=_cp("arbitrary", "arbitrary", "arbitrary"),
        name="diff_attn",
    )(lam2, qkv, qkv, qkv, subln_g.reshape(1, LANES))


def _na_bias_tables(rpb, rows):
    wr, wc = NA_WIN_ROWS, NA_WIN_COLS
    heads = rpb.shape[0]
    c = np.arange(GRID_W)[:, None]
    kc = np.arange(GRID_W)[None, :]
    cs = np.clip(c - wc // 2, 0, GRID_W - wc)
    cvalid = (kc >= cs) & (kc < cs + wc)
    onehot = ((kc - c + wc - 1)[None] == np.arange(2 * wc - 1)[:, None, None]) & cvalid[None]
    sel = jnp.asarray(onehot.reshape(2 * wc - 1, GRID_W * GRID_W), F32)
    toe = jnp.dot(rpb.astype(F32).reshape(heads * (2 * wr - 1), 2 * wc - 1), sel,
                  precision=lax.Precision.HIGHEST)
    toe = toe.reshape(heads, 2 * wr - 1, GRID_W, GRID_W) + jnp.asarray(np.where(cvalid, 0.0, NEG), F32)
    toe = jnp.concatenate([toe, jnp.full((heads, 1, GRID_W, GRID_W), NEG, F32)], axis=1)
    r0s = np.array([0, NA_QROWS, rows - NA_QROWS])
    ks = np.clip(r0s - wr // 2, 0, rows - NA_KROWS)
    r = r0s[:, None, None] + np.arange(NA_QROWS)[None, :, None]
    kr = ks[:, None, None] + np.arange(NA_KROWS)[None, None, :]
    rs = np.clip(r - wr // 2, 0, rows - wr)
    idx = np.where((kr >= rs) & (kr < rs + wr), kr - r + wr - 1, 2 * wr - 1)
    blocks = toe[:, idx]
    return blocks.transpose(0, 1, 2, 4, 3, 5).reshape(heads, 3, NA_QROWS * GRID_W, NA_KROWS * GRID_W)


def _na_kernel(q_ref, k_ref, v_ref, b_ref, o_ref, *, n_lat, n_ctx, rows):
    rb = pl.program_id(1)
    n_rb = rows // NA_QROWS
    nt = (((1,), (1,)), ((), ()))
    heads = [slice(hh * LANES, (hh + 1) * LANES) for hh in range(NA_HPS)]

    def ctx_scores(c):
        kc = k_ref[n_lat:n_lat + n_ctx, c]
        return lax.dot_general(q_ref[:, c], kc, nt, preferred_element_type=F32), v_ref[n_lat:n_lat + n_ctx, c]

    @pl.when(rb < n_rb)
    def _():
        ks = jnp.clip(rb * NA_QROWS - NA_WIN_ROWS // 2, 0, rows - NA_KROWS)
        start = pl.multiple_of(ks * GRID_W, GRID_W)
        for hh, c in enumerate(heads):
            s_ctx, vc = ctx_scores(c)
            kw = k_ref[pl.ds(start, NA_KROWS * GRID_W), c]
            vw = v_ref[pl.ds(start, NA_KROWS * GRID_W), c]
            s_loc = lax.dot_general(q_ref[:, c], kw, nt, preferred_element_type=F32) + b_ref[hh, 0]
            m = jnp.maximum(jnp.max(s_loc, axis=1, keepdims=True), jnp.max(s_ctx, axis=1, keepdims=True))
            p_loc = jnp.exp(s_loc - m)
            p_ctx = jnp.exp(s_ctx - m)
            l = jnp.sum(p_loc, axis=1, keepdims=True) + jnp.sum(p_ctx, axis=1, keepdims=True)
            o = (jnp.dot(p_ctx.astype(BF16), vc, preferred_element_type=F32)
                 + jnp.dot(p_loc.astype(BF16), vw, preferred_element_type=F32))
            o_ref[:, c] = (o / l).astype(o_ref.dtype)

    @pl.when(rb == n_rb)
    def _():
        for c in heads:
            s_ctx, vc = ctx_scores(c)
            m = jnp.max(s_ctx, axis=1, keepdims=True)
            p = jnp.exp(s_ctx - m)
            l = jnp.sum(p, axis=1, keepdims=True)
            o = jnp.dot(p.astype(BF16), vc, preferred_element_type=F32)
            o_ref[:, c] = (o / l).astype(o_ref.dtype)


def _na_attn(qkv, bias, n_lat, n_ctx, da_heads, heads):
    t = qkv.shape[0]
    rows = n_lat // GRID_W
    n_rb = rows // NA_QROWS
    tq = NA_QROWS * GRID_W
    assert tq == ROW_TILE and n_ctx == ROW_TILE
    assert heads % NA_HPS == 0 and (3 * da_heads) % NA_HPS == 0
    hg = heads // NA_HPS
    w = NA_HPS * LANES
    q0, k0, v0 = 3 * da_heads // NA_HPS, 3 * da_heads // NA_HPS + hg, 3 * da_heads // NA_HPS + 2 * hg

    def bias_map(h, rb):
        return (h, jnp.where(rb == 0, 0, jnp.where(rb == n_rb - 1, 2, 1)), 0, 0)

    return pl.pallas_call(
        functools.partial(_na_kernel, n_lat=n_lat, n_ctx=n_ctx, rows=rows),
        grid=(hg, n_rb + 1),
        in_specs=[pl.BlockSpec((tq, w), lambda h, rb: (rb, q0 + h)),
                  pl.BlockSpec((t, w), lambda h, rb: (0, k0 + h)),
                  pl.BlockSpec((t, w), lambda h, rb: (0, v0 + h)),
                  pl.BlockSpec((NA_HPS, 1, tq, NA_KROWS * GRID_W), bias_map)],
        out_specs=pl.BlockSpec((tq, w), lambda h, rb: (rb, h)),
        out_shape=jax.ShapeDtypeStruct((t, heads * LANES), BF16),
        compiler_params=_cp("arbitrary", "arbitrary"),
        name="na_attn",
    )(qkv, qkv, qkv, bias)


def _hgrn_prep(q, v, z, lb, rev):
    c = HG_CHUNK
    f = lb + (1.0 - lb) * jax.nn.sigmoid(z)
    kk = 1.0 - f
    g = jnp.log2(f)
    ri = lax.broadcasted_iota(jnp.int32, (c, c), 0)
    ci = lax.broadcasted_iota(jnp.int32, (c, c), 1)
    tri = ((ri <= ci) if rev else (ri >= ci)).astype(BF16)
    g_hi = g.astype(BF16)
    r1 = g - g_hi.astype(F32)
    g_mid = r1.astype(BF16)
    g_lo = (r1 - g_mid.astype(F32)).astype(BF16)
    c2 = jnp.dot(tri, jnp.concatenate([g_hi, g_mid], axis=1), preferred_element_type=F32)
    cum = c2[:, :LANES] + c2[:, LANES:] + jnp.dot(tri, g_lo, preferred_element_type=F32)
    total = jnp.sum(g, axis=0, keepdims=True)
    return dict(cum=cum, total=total, kk=kk, qs=_silu(q) * (LANES ** -0.5), v=v, vb=v.astype(BF16))


def _hgrn_intra(pp, rev):
    c, sub, nsub = HG_CHUNK, HG_SUB, HG_CHUNK // HG_SUB
    nt = (((1,), (1,)), ((), ()))
    cum, kk, qs, v, vb = pp["cum"], pp["kk"], pp["qs"], pp["v"], pp["vb"]
    ngrp = sub // SUBLANES
    ti = lax.broadcasted_iota(jnp.int32, (SUBLANES, LANES), 0)
    ckey = cum - jnp.log2(kk)
    o_blocks = [None] * nsub
    for i in range(nsub):
        lo = (nsub - 1 - i) * sub if rev else i * sub
        hi = lo + sub
        cum_i, q_i = cum[lo:hi], qs[lo:hi]
        og = [None] * ngrp
        if i > 0:
            brow = hi if rev else lo - 1
            bnd = cum[brow:brow + 1]
            elo, ehi = (hi, c) if rev else (0, lo)
            qt = (q_i * jnp.exp2(cum_i - bnd)).astype(BF16)
            kt = (kk[elo:ehi] * jnp.exp2(bnd - cum[elo:ehi])).astype(BF16)
            att = lax.dot_general(qt, kt, nt, preferred_element_type=F32)
            o_i = jnp.dot(att.astype(BF16), vb[elo:ehi], preferred_element_type=F32)
            og = [o_i[gq * SUBLANES:(gq + 1) * SUBLANES] for gq in range(ngrp)]
        zs, dst = [], []
        for s in range(sub):
            gs = s // SUBLANES
            for gq in (range(gs + 1) if rev else range(gs, ngrp)):
                r = slice(gq * SUBLANES, (gq + 1) * SUBLANES)
                dlt = cum_i[r] - ckey[lo + s:lo + s + 1]
                if gq == gs:
                    dlt = jnp.where((ti <= s - r.start) if rev else (ti >= s - r.start), dlt, NEG)
                zs.append(jnp.exp2(dlt) * q_i[r])
                dst.append((gq, s))
        a_all = jnp.sum(jnp.concatenate(zs, axis=0), axis=1, keepdims=True)
        for n, (gq, s) in enumerate(dst):
            term = a_all[n * SUBLANES:(n + 1) * SUBLANES] * v[lo + s:lo + s + 1]
            og[gq] = term if og[gq] is None else og[gq] + term
        o_blocks[(nsub - 1 - i) if rev else i] = jnp.concatenate(og, axis=0)
    return jnp.concatenate(o_blocks, axis=0)


def _hgrn_carry(pp, st):
    cum, total = pp["cum"], pp["total"]
    o_inter = lax.dot_general((pp["qs"] * jnp.exp2(cum)).astype(BF16), st.astype(BF16),
                              (((1,), (1,)), ((), ())), preferred_element_type=F32)
    kdec = (pp["kk"] * jnp.exp2(total - cum)).astype(BF16)
    st_new = st * jnp.exp2(total) + lax.dot_general(pp["vb"], kdec, (((0,), (0,)), ((), ())),
                                                    preferred_element_type=F32)
    return o_inter, st_new


def _hgrn_kernel(*refs, rev, final, n_chunks):
    if final:
        q_ref, v_ref, z_ref, lb_ref, of_ref, gate_ref, gn_ref, o_ref, st_ref = refs
    else:
        q_ref, v_ref, z_ref, lb_ref, o_ref, st_ref = refs

    @pl.when(pl.program_id(1) == 0)
    def _():
        st_ref[...] = jnp.zeros(st_ref.shape, F32)

    order = [(n_chunks - 1 - s) if rev else s for s in range(n_chunks)]
    tiles = [(slice(ch * HG_CHUNK, (ch + 1) * HG_CHUNK), hh, slice(hh * LANES, (hh + 1) * LANES))
             for ch in order for hh in range(HG_HPS)]
    preps = [_hgrn_prep(q_ref[rows, cols], v_ref[rows, cols], z_ref[rows, cols], lb_ref[:, cols], rev)
             for rows, _, cols in tiles]
    intras = [_hgrn_intra(pp, rev) for pp in preps]
    states = [st_ref[hh] for hh in range(HG_HPS)]
    for (rows, hh, cols), pp, o in zip(tiles, preps, intras):
        o_inter, states[hh] = _hgrn_carry(pp, states[hh])
        o = o + o_inter
        if final:
            o = _rms(o + of_ref[rows, cols], gn_ref[...]) * _silu(gate_ref[rows, cols])
        o_ref[rows, cols] = o.astype(o_ref.dtype)
    for hh in range(HG_HPS):
        st_ref[hh] = states[hh]


def _hgrn_dir(p, lb_dir, rev, n_lat, heads, o_fwd=None, gnorm_g=None):
    t = p.shape[0]
    d = heads * LANES
    n_blk = t // ROW_TILE
    last = n_blk - 1
    final = o_fwd is not None

    def tok(j):
        return jnp.where(j == 0, last, (last - j) if rev else (j - 1))

    assert heads % HG_HPS == 0
    hg = heads // HG_HPS
    w = HG_HPS * LANES
    zcol = 3 * hg if rev else 2 * hg
    in_specs = [pl.BlockSpec((ROW_TILE, w), lambda h, j: (tok(j), h)),
                pl.BlockSpec((ROW_TILE, w), lambda h, j: (tok(j), hg + h)),
                pl.BlockSpec((ROW_TILE, w), lambda h, j: (tok(j), zcol + h)),
                pl.BlockSpec((1, w), lambda h, j: (0, h))]
    args = [p, p, p, lb_dir.reshape(1, d)]
    if final:
        in_specs += [pl.BlockSpec((ROW_TILE, w), lambda h, j: (tok(j), h)),
                     pl.BlockSpec((ROW_TILE, w), lambda h, j: (tok(j), 4 * hg + h)),
                     pl.BlockSpec((1, LANES), lambda h, j: (0, 0))]
        args += [o_fwd, p, gnorm_g.reshape(1, LANES)]
    return pl.pallas_call(
        functools.partial(_hgrn_kernel, rev=rev, final=final, n_chunks=ROW_TILE // HG_CHUNK),
        grid=(hg, n_blk),
        in_specs=in_specs,
        out_specs=pl.BlockSpec((ROW_TILE, w), lambda h, j: (tok(j), h)),
        out_shape=jax.ShapeDtypeStruct((t, d), BF16 if final else F32),
        scratch_shapes=[pltpu.VMEM((HG_HPS, LANES, LANES), F32)],
        compiler_params=_cp("arbitrary", "arbitrary"),
        name="hgrn_bwd" if rev else "hgrn_fwd",
    )(*args)


def _new_expert(be_ref):
    b = pl.program_id(0)
    return jnp.logical_or(b == 0, be_ref[b] != be_ref[jnp.maximum(b - 1, 0)])


def _expert_weights(be_ref, nu_ref, ws_ref, nx_ref, w_hbms, wbuf, wsem, wb_refs, layer):
    b = pl.program_id(0)

    def copies(expert, slot):
        return [pltpu.make_async_copy(w.at[layer, expert], wbuf.at[slot, i], wsem.at[slot])
                for i, w in enumerate(w_hbms)]

    @pl.when(b == 0)
    def _():
        for cp in copies(be_ref[0], ws_ref[0]):
            cp.start(priority=1)

    @pl.when(jnp.logical_and(b < nu_ref[0], _new_expert(be_ref)))
    def _():
        slot = ws_ref[b]
        for cp in copies(be_ref[b], slot):
            cp.wait()

        @pl.when(nx_ref[b] >= 0)
        def _():
            for cp in copies(nx_ref[b], 1 - slot):
                cp.start(priority=1)

        for i, wb in enumerate(wb_refs):
            wb[...] = wbuf[slot, i].astype(BF16)


def _moe_up_kernel(be_ref, nu_ref, ws_ref, nx_ref, base_ref, nval_ref, tok_ref, x_hbm, wg_hbm, wu_hbm, o_ref,
                   wgb_ref, wub_ref, xbuf, sem, wbuf, wsem, *, layer):
    b = pl.program_id(0)
    nch = wgb_ref.shape[0] // LANES

    def gather(blk, start):
        base = base_ref[blk]
        last = jnp.maximum(nval_ref[blk] - 1, 0)
        _row_gather(lambda r: tok_ref[base + jnp.minimum(r, last)], x_hbm, xbuf.at[blk % 2], sem.at[blk % 2],
                    MOE_BM, start, span=nch)

    @pl.when(b == 0)
    def _():
        gather(b, True)

    @pl.when(b + 1 < nu_ref[0])
    def _():
        gather(b + 1, True)

    _expert_weights(be_ref, nu_ref, ws_ref, nx_ref, (wg_hbm, wu_hbm), wbuf, wsem, (wgb_ref, wub_ref), layer)

    @pl.when(b < nu_ref[0])
    def _():
        gather(b, False)
        xb = xbuf.at[b % 2]
        x = jnp.concatenate([xb[pl.ds(j, MOE_BM, stride=nch), :] for j in range(nch)], axis=1).astype(BF16)
        hg = jnp.dot(x, wgb_ref[...], preferred_element_type=F32)
        hu = jnp.dot(x, wub_ref[...], preferred_element_type=F32)
        o_ref[...] = (_silu(hg) * hu).astype(o_ref.dtype)

    @pl.when(b >= nu_ref[0])
    def _():
        o_ref[...] = jnp.zeros(o_ref.shape, o_ref.dtype)


def _moe_down_kernel(be_ref, nu_ref, ws_ref, nx_ref, h_ref, wd_hbm, o_ref, wdb_ref, wbuf, wsem, *, layer):
    b = pl.program_id(0)
    _expert_weights(be_ref, nu_ref, ws_ref, nx_ref, (wd_hbm,), wbuf, wsem, (wdb_ref,), layer)

    @pl.when(b < nu_ref[0])
    def _():
        o_ref[...] = jnp.dot(h_ref[...], wdb_ref[...], preferred_element_type=F32)

    @pl.when(b >= nu_ref[0])
    def _():
        o_ref[...] = jnp.zeros(o_ref.shape, o_ref.dtype)


def _moe_experts(m, tok_sorted, blk_expert, blk_base, blk_nval, n_used, blk_wslot, blk_next, w_gate, w_up,
                 w_down, layer):
    d = w_gate.shape[2]
    n_blk = blk_expert.shape[0]
    n_slot = n_blk * MOE_BM
    de = w_gate.shape[3]
    hbm = pl.BlockSpec(memory_space=pl.ANY)
    hid = pl.pallas_call(
        functools.partial(_moe_up_kernel, layer=layer),
        grid_spec=pltpu.PrefetchScalarGridSpec(
            num_scalar_prefetch=7, grid=(n_blk,),
            in_specs=[hbm, hbm, hbm],
            out_specs=pl.BlockSpec((MOE_BM, de), lambda b, *_: (b, 0)),
            scratch_shapes=[pltpu.VMEM((d, de), BF16), pltpu.VMEM((d, de), BF16),
                            pltpu.VMEM((2, MOE_BM * (d // LANES), LANES), F32), pltpu.SemaphoreType.DMA((2,)),
                            pltpu.VMEM((2, 2, d, de), F32), pltpu.SemaphoreType.DMA((2,))]),
        out_shape=jax.ShapeDtypeStruct((n_slot, de), BF16),
        compiler_params=_cp("arbitrary"),
        name="moe_up",
    )(blk_expert, n_used, blk_wslot, blk_next, blk_base, blk_nval, tok_sorted, m, w_gate, w_up)
    return pl.pallas_call(
        functools.partial(_moe_down_kernel, layer=layer),
        grid_spec=pltpu.PrefetchScalarGridSpec(
            num_scalar_prefetch=4, grid=(n_blk,),
            in_specs=[pl.BlockSpec((MOE_BM, de), lambda b, *_: (b, 0)), hbm],
            out_specs=pl.BlockSpec((MOE_BM, d), lambda b, *_: (b, 0)),
            scratch_shapes=[pltpu.VMEM((de, d), BF16), pltpu.VMEM((2, 1, de, d), F32),
                            pltpu.SemaphoreType.DMA((2,))]),
        out_shape=jax.ShapeDtypeStruct((n_slot, d), F32),
        compiler_params=_cp("arbitrary"),
        name="moe_down",
    )(blk_expert, n_used, blk_wslot, blk_next, hid, w_down)


def _route(logits):
    n_exp = N_GROUPS * EXPERTS_PER_GROUP
    t = logits.shape[0]
    g_logits = logits[:, :N_GROUPS]
    g_prob = jax.nn.softmax(g_logits, axis=-1)
    g_sel = jnp.argmax(g_logits, axis=-1).astype(jnp.int32)
    g_w = jnp.take_along_axis(g_prob, g_sel[:, None], axis=1)[:, 0]
    e_logits = logits[:, N_GROUPS:N_GROUPS + n_exp].reshape(t, N_GROUPS, EXPERTS_PER_GROUP)
    e_logits = jnp.take_along_axis(e_logits, g_sel[:, None, None], axis=1)[:, 0]
    top_v, top_i = lax.top_k(e_logits, TOP_K)
    weights = g_w[:, None] * jax.nn.softmax(top_v, axis=-1)
    expert = g_sel[:, None] * EXPERTS_PER_GROUP + top_i.astype(jnp.int32)
    return expert, weights


def _dispatch(expert, weights):
    n_exp = N_GROUPS * EXPERTS_PER_GROUP
    t = expert.shape[0]
    n_as = t * TOP_K
    eid = expert.reshape(-1)
    aidx = jnp.arange(n_as, dtype=jnp.int32)
    eid_s, order = lax.sort((eid, aidx), num_keys=1, is_stable=True)
    onehot = eid_s[:, None] == jnp.arange(n_exp, dtype=jnp.int32)[None, :]
    counts = jnp.sum(onehot, axis=0, dtype=jnp.int32)
    padded = (counts + MOE_BM - 1) // MOE_BM * MOE_BM
    start = jnp.cumsum(counts) - counts
    pend = jnp.cumsum(padded)
    pstart = pend - padded
    dest = aidx + jnp.sum(jnp.where(onehot, (pstart - start)[None, :], 0), axis=1)
    n_blk = -(-n_as // MOE_BM) + n_exp
    blk_expert = jnp.searchsorted(pend, jnp.arange(n_blk, dtype=jnp.int32) * MOE_BM, side='right')
    blk_expert = jnp.minimum(blk_expert, n_exp - 1).astype(jnp.int32)
    n_used = (pend[-1] // MOE_BM).astype(jnp.int32).reshape(1)
    blk_first = jnp.arange(n_blk, dtype=jnp.int32) * MOE_BM - pstart[blk_expert]
    blk_base = jnp.clip(start[blk_expert] + blk_first, 0, n_as - 1).astype(jnp.int32)
    blk_nval = jnp.clip(counts[blk_expert] - blk_first, 0, MOE_BM).astype(jnp.int32)
    blk_base = jnp.minimum(blk_base, n_as - jnp.maximum(blk_nval, 1))
    _, slot_of = lax.sort((order, dest), num_keys=1)
    slot_idx = slot_of.reshape(t, TOP_K).T.reshape(-1)
    used = counts > 0
    ids = jnp.arange(n_exp, dtype=jnp.int32)
    wslot_e = (jnp.cumsum(used.astype(jnp.int32)) - 1) % 2
    later = jnp.where(used[None, :] & (ids[None, :] > ids[:, None]), ids[None, :], n_exp)
    next_e = jnp.min(later, axis=1)
    next_e = jnp.where(next_e == n_exp, -1, next_e).astype(jnp.int32)
    blk_wslot = jnp.maximum(wslot_e, 0)[blk_expert].astype(jnp.int32)
    blk_next = next_e[blk_expert]
    return order // TOP_K, blk_expert, blk_base, blk_nval, n_used, blk_wslot, blk_next, slot_idx


def _moe(m, logits, w_gate, w_up, w_down, layer):
    expert, weights = _route(logits)
    tok_sorted, blk_expert, blk_base, blk_nval, n_used, blk_wslot, blk_next, slot_idx = _dispatch(expert, weights)
    ys = _moe_experts(m, tok_sorted, blk_expert, blk_base, blk_nval, n_used, blk_wslot, blk_next,
                      w_gate, w_up, w_down, layer)
    return ys, slot_idx, weights.T[:, :, None]


def _diff_lambda(lam_p, layer_idx):
    lam_init = 0.8 - 0.6 * math.exp(-0.3 * layer_idx)
    lf = lam_p.astype(F32)
    lam = jnp.exp(jnp.sum(lf[0] * lf[1])) - jnp.exp(jnp.sum(lf[2] * lf[3])) + lam_init
    return jnp.stack([lam, jnp.asarray(1.0 - lam_init, F32)]).astype(F32)


def kernel(x, c, ctx, c_ctx, w_mod, b_mod, norm1_g, norm2_g, att_w_in, att_w_out, att_lambda, att_subln_g,
           att_rpb, rec_w_in, rec_w_out, rec_lb_logits, rec_gnorm_g, moe_w_group, moe_b_group, moe_w_router,
           moe_b_router, moe_w_gate, moe_w_up, moe_w_down, final_norm_g):
    b_, n_lat, d = x.shape
    n_ctx = ctx.shape[1]
    assert b_ == 1 and n_ctx == ROW_TILE and n_lat % ROW_TILE == 0 and d % LANES == 0
    depth = w_mod.shape[0]
    daw, naw = DA_HEADS * LANES, NA_HEADS * LANES
    n_exp = N_GROUPS * EXPERTS_PER_GROUP

    lbp = jax.nn.softmax(rec_lb_logits.astype(F32), axis=0)
    lbs = jnp.cumsum(lbp, axis=0) - lbp[0:1]

    cvec = jnp.zeros((8, d), F32).at[0].set(c[0]).at[1].set(c_ctx)
    mods = _modulation(cvec, w_mod, b_mod)
    cos, sin = _rope_tables(n_lat, n_ctx)

    h = jnp.concatenate([x[0], ctx[0]], axis=0)
    a = _ln_mod(h, norm1_g[0], mods[0], 0, 1, n_lat)
    out = None
    for l in range(depth):
        j = l // 2
        mod = mods[l]
        if l % 2 == 0:
            qkv = _attn_in_proj(a, att_w_in, j, cos, sin, daw, naw)
            lam2 = _diff_lambda(att_lambda[j], l)
            da_lat = _diff_attn(qkv, lam2, att_subln_g[j], 0, n_lat, 0, n_lat + n_ctx, DA_HEADS)
            da_ctx = _diff_attn(qkv, lam2, att_subln_g[j], n_lat, n_ctx, n_lat, n_ctx, DA_HEADS)
            bias = _na_bias_tables(att_rpb[j], n_lat // GRID_W)
            na = _na_attn(qkv, bias, n_lat, n_ctx, DA_HEADS, NA_HEADS)
            y = _matmul(jnp.concatenate([da_lat, da_ctx], axis=0), att_w_out, j, F32, a2=na)
        else:
            p = _matmul(a, rec_w_in, j, F32)
            o_f = _hgrn_dir(p, lbs[l, 0], False, n_lat, HG_HEADS)
            o = _hgrn_dir(p, lbs[l, 1], True, n_lat, HG_HEADS, o_fwd=o_f, gnorm_g=rec_gnorm_g[j])
            y = _matmul(o, rec_w_out, j, F32)
        w_route = jnp.zeros((d, LANES), F32).at[:, :N_GROUPS].set(moe_w_group[l])
        w_route = w_route.at[:, N_GROUPS:N_GROUPS + n_exp].set(moe_w_router[l])
        b_route = jnp.zeros((1, LANES), F32).at[0, :N_GROUPS].set(moe_b_group[l])
        b_route = b_route.at[0, N_GROUPS:N_GROUPS + n_exp].set(moe_b_router[l])
        h, m, logits = _res_ln_mod_route(h, y, norm2_g[l], mod, 2, 3, 4, w_route, b_route, n_lat)
        ys, slot_idx, wt = _moe(m, logits, moe_w_gate, moe_w_up, moe_w_down, l)
        if l + 1 < depth:
            h, a = _res_ln_mod(h, ys, slot_idx, wt, norm1_g[l + 1], mod, 5, mods[l + 1], 0, 1, n_lat)
        else:
            out = _res_final(h, ys, slot_idx, wt, final_norm_g, mod, 5, n_lat)
    return out[None]
```

```python
import functools
import math

import numpy as np
import jax
import jax.numpy as jnp
from jax import lax
from jax.experimental import pallas as pl
from jax.experimental.pallas import tpu as pltpu

F32 = jnp.float32
BF16 = jnp.bfloat16

GRID_W = 64
EPS = 1e-6
DA_HEADS = 8
DA_QK_DIM = 64
NA_HEADS = 8
NA_WIN_ROWS = 8
NA_WIN_COLS = 16
ROPE_THETA = 10000.0
HG_HEADS = 16
N_GROUPS = 4
EXPERTS_PER_GROUP = 8
TOP_K = 2

LANES = 128
SUBLANES = 8
ROW_TILE = 256
HG_CHUNK = 64
HG_SUB = 32
HG_HPS = 2
NA_QROWS = 4
NA_KROWS = 12
NA_HPS = 4
MOE_BM = 256
NEG = -1e30
VMEM_LIMIT = 56 << 20


def _cp(*sem):
    return pltpu.CompilerParams(dimension_semantics=sem, vmem_limit_bytes=VMEM_LIMIT)


def _pick(n, cands):
    for c in cands:
        if n % c == 0:
            return c
    raise ValueError(f"no tile for {n} in {cands}")


def _silu(x):
    return x * jax.nn.sigmoid(x)


def _rms(x, g):
    return x * lax.rsqrt(jnp.mean(x * x, axis=-1, keepdims=True) + EPS) * g


def _mod_kernel(c_ref, w_ref, b_ref, o_ref):
    s = _silu(c_ref[...]).astype(BF16)
    o_ref[0] = jnp.dot(s, w_ref[0].astype(BF16), preferred_element_type=F32) + b_ref[0]


def _modulation(cvec, w_mod, b_mod):
    depth, d, n = w_mod.shape
    tn = _pick(n, (1024, 512, 256, 128))
    return pl.pallas_call(
        _mod_kernel,
        grid=(depth, n // tn),
        in_specs=[pl.BlockSpec((8, d), lambda l, j: (0, 0)),
                  pl.BlockSpec((1, d, tn), lambda l, j: (l, 0, j)),
                  pl.BlockSpec((1, 1, tn), lambda l, j: (l, 0, j))],
        out_specs=pl.BlockSpec((1, 8, tn), lambda l, j: (l, 0, j)),
        out_shape=jax.ShapeDtypeStruct((depth, 8, n), F32),
        compiler_params=_cp("arbitrary", "arbitrary"),
        name="modulation",
    )(cvec, w_mod, b_mod.reshape(depth, 1, n))


def _mod_row(ref, n_lat_tiles):
    r = (pl.program_id(0) >= n_lat_tiles).astype(jnp.int32)
    return ref[pl.ds(r, 1), :]


def _ln_mod_kernel(h_ref, g_ref, sh_ref, sc_ref, a_ref, *, n_lat_tiles):
    y = _rms(h_ref[...], g_ref[...])
    a = y * (1.0 + _mod_row(sc_ref, n_lat_tiles)) + _mod_row(sh_ref, n_lat_tiles)
    a_ref[...] = a.astype(a_ref.dtype)


def _row_gather(row_of, src_hbm, dst, sem, n_rows, start, span=1):
    def body(r, carry):
        cp = pltpu.make_async_copy(src_hbm.at[pl.ds(pl.multiple_of(row_of(r) * span, span), span)],
                                   dst.at[pl.ds(pl.multiple_of(r * span, span), span)], sem)
        if start:
            cp.start()
        else:
            cp.wait()
        return carry

    if isinstance(n_rows, int):
        lax.fori_loop(0, n_rows, body, 0, unroll=8)
    else:
        lax.fori_loop(0, n_rows, body, 0)


def _combined_expert_rows(idx_ref, wt_ref, ys_hbm, ybuf, sem, t):
    i = pl.program_id(0)

    def gather(tile, start):
        slot = tile % 2
        for k in range(TOP_K):
            base = k * t + tile * ROW_TILE
            _row_gather(lambda r: idx_ref[base + r], ys_hbm, ybuf.at[slot, k], sem.at[slot], ROW_TILE, start)

    @pl.when(i == 0)
    def _():
        gather(i, True)

    @pl.when(i + 1 < pl.num_programs(0))
    def _():
        gather(i + 1, True)

    gather(i, False)
    y = wt_ref[0] * ybuf[i % 2, 0]
    for k in range(1, TOP_K):
        y = y + wt_ref[k] * ybuf[i % 2, k]
    return y


def _res_ln_mod_kernel(idx_ref, h_ref, wt_ref, ys_hbm, gt_ref, g_ref, sh_ref, sc_ref, ho_ref, a_ref, ybuf, sem,
                       *, n_lat_tiles, t):
    y = _combined_expert_rows(idx_ref, wt_ref, ys_hbm, ybuf, sem, t)
    h = h_ref[...] + _mod_row(gt_ref, n_lat_tiles) * y
    ho_ref[...] = h
    a = _rms(h, g_ref[...]) * (1.0 + _mod_row(sc_ref, n_lat_tiles)) + _mod_row(sh_ref, n_lat_tiles)
    a_ref[...] = a.astype(a_ref.dtype)


def _res_ln_mod_route_kernel(h_ref, y_ref, gt_ref, g_ref, sh_ref, sc_ref, wr_ref, br_ref,
                             ho_ref, a_ref, lg_ref, *, n_lat_tiles):
    h = h_ref[...] + _mod_row(gt_ref, n_lat_tiles) * y_ref[...]
    ho_ref[...] = h
    a = _rms(h, g_ref[...]) * (1.0 + _mod_row(sc_ref, n_lat_tiles)) + _mod_row(sh_ref, n_lat_tiles)
    nch = a.shape[1] // LANES
    for j in range(nch):
        a_ref[pl.ds(j, ROW_TILE, stride=nch), :] = a[:, j * LANES:(j + 1) * LANES]
    lg_ref[...] = jnp.dot(a, wr_ref[...], preferred_element_type=F32,
                          precision=lax.Precision.HIGHEST) + br_ref[...]


def _res_final_kernel(idx_ref, h_ref, wt_ref, ys_hbm, gt_ref, g_ref, o_ref, ybuf, sem, *, t):
    y = _combined_expert_rows(idx_ref, wt_ref, ys_hbm, ybuf, sem, t)
    h = h_ref[...] + gt_ref[pl.ds(0, 1), :] * y
    o_ref[...] = _rms(h, g_ref[...])


def _row_spec(d):
    return pl.BlockSpec((ROW_TILE, d), lambda i, *_: (i, 0))


def _vec_spec(d):
    return pl.BlockSpec((1, d), lambda i, *_: (0, 0))


def _mod_spec(d, k):
    return pl.BlockSpec((8, d), lambda i, *_: (0, k))


def _combine_scratch(d):
    return [pltpu.VMEM((2, TOP_K, ROW_TILE, d), F32), pltpu.SemaphoreType.DMA((2,))]


def _wt_spec():
    return pl.BlockSpec((TOP_K, ROW_TILE, 1), lambda i, *_: (0, i, 0))


def _ln_mod(h, g, mod, k_shift, k_scale, n_lat):
    t, d = h.shape
    return pl.pallas_call(
        functools.partial(_ln_mod_kernel, n_lat_tiles=n_lat // ROW_TILE),
        grid=(t // ROW_TILE,),
        in_specs=[_row_spec(d), _vec_spec(d), _mod_spec(d, k_shift), _mod_spec(d, k_scale)],
        out_specs=_row_spec(d),
        out_shape=jax.ShapeDtypeStruct((t, d), BF16),
        compiler_params=_cp("arbitrary"),
        name="ln_mod",
    )(h, g.reshape(1, d), mod, mod)


def _res_ln_mod(h, ys, slot_idx, wt, g, mod_gate, k_gate, mod_next, k_shift, k_scale, n_lat):
    t, d = h.shape
    return pl.pallas_call(
        functools.partial(_res_ln_mod_kernel, n_lat_tiles=n_lat // ROW_TILE, t=t),
        grid_spec=pltpu.PrefetchScalarGridSpec(
            num_scalar_prefetch=1, grid=(t // ROW_TILE,),
            in_specs=[_row_spec(d), _wt_spec(), pl.BlockSpec(memory_space=pl.ANY), _mod_spec(d, k_gate),
                      _vec_spec(d), _mod_spec(d, k_shift), _mod_spec(d, k_scale)],
            out_specs=[_row_spec(d), _row_spec(d)],
            scratch_shapes=_combine_scratch(d)),
        out_shape=[jax.ShapeDtypeStruct((t, d), F32), jax.ShapeDtypeStruct((t, d), BF16)],
        compiler_params=_cp("arbitrary"),
        name="res_ln_mod",
    )(slot_idx, h, wt, ys, mod_gate, g.reshape(1, d), mod_next, mod_next)


def _res_ln_mod_route(h, y, g, mod, k_gate, k_shift, k_scale, w_route, b_route, n_lat):
    t, d = h.shape
    return pl.pallas_call(
        functools.partial(_res_ln_mod_route_kernel, n_lat_tiles=n_lat // ROW_TILE),
        grid=(t // ROW_TILE,),
        in_specs=[_row_spec(d), _row_spec(d), _mod_spec(d, k_gate), _vec_spec(d),
                  _mod_spec(d, k_shift), _mod_spec(d, k_scale),
                  pl.BlockSpec((d, LANES), lambda i: (0, 0)), _vec_spec(LANES)],
        out_specs=[_row_spec(d), pl.BlockSpec((ROW_TILE * (d // LANES), LANES), lambda i: (i, 0)),
                   _row_spec(LANES)],
        out_shape=[jax.ShapeDtypeStruct((t, d), F32), jax.ShapeDtypeStruct((t * (d // LANES), LANES), F32),
                   jax.ShapeDtypeStruct((t, LANES), F32)],
        compiler_params=_cp("arbitrary"),
        name="res_ln_mod_route",
    )(h, y, mod, g.reshape(1, d), mod, mod, w_route, b_route)


def _res_final(h, ys, slot_idx, wt, g, mod, k_gate, n_lat):
    t, d = h.shape
    return pl.pallas_call(
        functools.partial(_res_final_kernel, t=t),
        grid_spec=pltpu.PrefetchScalarGridSpec(
            num_scalar_prefetch=1, grid=(n_lat // ROW_TILE,),
            in_specs=[_row_spec(d), _wt_spec(), pl.BlockSpec(memory_space=pl.ANY), _mod_spec(d, k_gate),
                      _vec_spec(d)],
            out_specs=_row_spec(d),
            scratch_shapes=_combine_scratch(d)),
        out_shape=jax.ShapeDtypeStruct((n_lat, d), F32),
        compiler_params=_cp("arbitrary"),
        name="res_final",
    )(slot_idx, h, wt, ys, mod, g.reshape(1, d))


def _matmul_kernel(a_ref, w_ref, o_ref, wb_ref):
    @pl.when(pl.program_id(1) == 0)
    def _():
        wb_ref[...] = w_ref[0].astype(BF16)

    o_ref[...] = jnp.dot(a_ref[...], wb_ref[...], preferred_element_type=F32).astype(o_ref.dtype)


def _matmul2_kernel(a1_ref, a2_ref, w_ref, o_ref, wb_ref):
    @pl.when(pl.program_id(1) == 0)
    def _():
        wb_ref[...] = w_ref[0].astype(BF16)

    k1 = a1_ref.shape[1]
    o_ref[...] = (jnp.dot(a1_ref[...], wb_ref[:k1, :], preferred_element_type=F32)
                  + jnp.dot(a2_ref[...], wb_ref[k1:, :], preferred_element_type=F32)).astype(o_ref.dtype)


def _matmul(a, w_stack, layer, out_dtype, a2=None):
    m = a.shape[0]
    ops = [a] if a2 is None else [a, a2]
    k = w_stack.shape[1]
    n = w_stack.shape[2]
    assert sum(x.shape[1] for x in ops) == k
    tm = _pick(m, (768, 640, 512, 256))
    tn = _pick(n, (1024, 512, 256, 128))
    return pl.pallas_call(
        _matmul_kernel if a2 is None else _matmul2_kernel,
        grid=(n // tn, m // tm),
        in_specs=[pl.BlockSpec((tm, x.shape[1]), lambda j, i: (i, 0)) for x in ops]
        + [pl.BlockSpec((1, k, tn), lambda j, i: (layer, 0, j))],
        out_specs=pl.BlockSpec((tm, tn), lambda j, i: (i, j)),
        out_shape=jax.ShapeDtypeStruct((m, n), out_dtype),
        scratch_shapes=[pltpu.VMEM((k, tn), BF16)],
        compiler_params=_cp("arbitrary", "arbitrary"),
        name="matmul",
    )(*ops, w_stack)


def _attn_in_kernel(a_ref, w_ref, cos_ref, sin_ref, o_ref, wb_ref, *, daw, naw, sa, sn):
    @pl.when(pl.program_id(1) == 0)
    def _():
        wb_ref[...] = w_ref[0].astype(BF16)

    tn = o_ref.shape[1]
    c0 = pl.program_id(0) * tn
    p = jnp.dot(a_ref[...], wb_ref[...], preferred_element_type=F32)

    @pl.when(c0 < 2 * daw)
    def _():
        cos = cos_ref[...]
        sin = sin_ref[...]
        lane = lax.broadcasted_iota(jnp.int32, cos.shape, 1)
        first_half = (lane % (DA_QK_DIM // 2)) < (DA_QK_DIM // 4)
        quarter = DA_QK_DIM // 4
        scale = jnp.where(c0 < daw, sa, 1.0).astype(F32)
        for g in range(tn // LANES):
            x = p[:, g * LANES:(g + 1) * LANES]
            partner = jnp.where(first_half, pltpu.roll(x, LANES - quarter, 1), pltpu.roll(x, quarter, 1))
            o_ref[:, g * LANES:(g + 1) * LANES] = ((x * cos + partner * sin) * scale).astype(o_ref.dtype)

    @pl.when(c0 >= 2 * daw)
    def _():
        scale = jnp.where(jnp.logical_and(c0 >= 3 * daw, c0 < 3 * daw + naw), sn, 1.0).astype(F32)
        o_ref[...] = (p * scale).astype(o_ref.dtype)


def _rope_tables(n_lat, n_ctx):
    t = jnp.arange(n_lat, dtype=jnp.int32)
    rows = (t // GRID_W).astype(F32)
    cols = (t % GRID_W).astype(F32)
    n_freq = DA_QK_DIM // 4
    inv_freq = ROPE_THETA ** (-jnp.arange(n_freq, dtype=F32) / n_freq)
    ang_r = rows[:, None] * inv_freq
    ang_c = cols[:, None] * inv_freq
    ang = jnp.concatenate([ang_r, ang_r, ang_c, ang_c], axis=1)
    sign = jnp.concatenate([-jnp.ones((n_freq,), F32), jnp.ones((n_freq,), F32)] * 2)
    cos = jnp.cos(ang)
    sin = jnp.sin(ang) * sign
    reps = LANES // DA_QK_DIM
    cos = jnp.concatenate([jnp.tile(cos, (1, reps)), jnp.ones((n_ctx, LANES), F32)], axis=0)
    sin = jnp.concatenate([jnp.tile(sin, (1, reps)), jnp.zeros((n_ctx, LANES), F32)], axis=0)
    return cos, sin


def _attn_in_proj(a, w_stack, layer, cos, sin, daw, naw):
    m, k = a.shape
    n = w_stack.shape[2]
    tm = _pick(m, (768, 640, 512, 256))
    tn = _pick(math.gcd(daw, naw), (1024, 512, 256, 128))
    kern = functools.partial(_attn_in_kernel, daw=daw, naw=naw, sa=DA_QK_DIM ** -0.5 * math.log2(math.e),
                             sn=LANES ** -0.5)
    return pl.pallas_call(
        kern,
        grid=(n // tn, m // tm),
        in_specs=[pl.BlockSpec((tm, k), lambda j, i: (i, 0)),
                  pl.BlockSpec((1, k, tn), lambda j, i: (layer, 0, j)),
                  pl.BlockSpec((tm, LANES), lambda j, i: (i, 0)),
                  pl.BlockSpec((tm, LANES), lambda j, i: (i, 0))],
        out_specs=pl.BlockSpec((tm, tn), lambda j, i: (i, j)),
        out_shape=jax.ShapeDtypeStruct((m, n), BF16),
        scratch_shapes=[pltpu.VMEM((k, tn), BF16)],
        compiler_params=_cp("arbitrary", "arbitrary"),
        name="attn_in_proj",
    )(a, w_stack, cos, sin)


def _diff_attn_kernel(lam_ref, q_ref, k_ref, v_ref, g_ref, o_ref, qm_ref, m_ref, acc_ref, *, nk, rg):
    kv = pl.program_id(2)
    tq = q_ref.shape[0]

    @pl.when(kv == 0)
    def _():
        q = q_ref[...]
        lane = lax.broadcasted_iota(jnp.int32, q.shape, 1)
        zero = jnp.zeros_like(q)
        qm_ref[0] = jnp.where(lane < DA_QK_DIM, q, zero)
        qm_ref[1] = jnp.where(lane >= DA_QK_DIM, q, zero)
        m_ref[...] = jnp.full(m_ref.shape, NEG, F32)
        acc_ref[...] = jnp.zeros(acc_ref.shape, F32)

    k = k_ref[...]
    v = v_ref[...]
    vext = jnp.concatenate([v, jnp.ones(v.shape, v.dtype)], axis=1)
    for r0 in range(0, tq, rg):
        for mi in range(2):
            s = lax.dot_general(qm_ref[mi, r0:r0 + rg], k, (((1,), (1,)), ((), ())), preferred_element_type=F32)
            m_prev = m_ref[mi, r0:r0 + rg]
            m_new = jnp.maximum(m_prev, jnp.max(s, axis=1, keepdims=True))
            p = jnp.exp2(s - m_new)
            acc_ref[mi, r0:r0 + rg] = (jnp.exp2(m_prev - m_new) * acc_ref[mi, r0:r0 + rg]
                                       + jnp.dot(p.astype(BF16), vext, preferred_element_type=F32))
            m_ref[mi, r0:r0 + rg] = m_new

    @pl.when(kv == nk - 1)
    def _():
        a0 = acc_ref[0]
        a1 = acc_ref[1]
        o = a0[:, :LANES] / a0[:, LANES:LANES + 1] - lam_ref[0] * (a1[:, :LANES] / a1[:, LANES:LANES + 1])
        o_ref[...] = (_rms(o, g_ref[...]) * lam_ref[1]).astype(o_ref.dtype)


DA_TQ, DA_TK, DA_RG = 1024, 2816, 256


def _diff_attn(qkv, lam2, subln_g, q_row0, n_q, k_row0, n_k, heads):
    tq = _pick(n_q, (DA_TQ, 1024, 512, 256))
    tk = _pick(n_k, (DA_TK, 1408, 1280, 1024, 768, 512, 256))
    assert q_row0 % tq == 0 and k_row0 % tk == 0
    nq, nk = n_q // tq, n_k // tk
    qb, kb = q_row0 // tq, k_row0 // tk
    return pl.pallas_call(
        functools.partial(_diff_attn_kernel, nk=nk, rg=min(DA_RG, tq)),
        grid=(heads, nq, nk),
        in_specs=[pl.BlockSpec(memory_space=pltpu.SMEM),
                  pl.BlockSpec((tq, LANES), lambda h, i, j: (qb + i, h)),
                  pl.BlockSpec((tk, LANES), lambda h, i, j: (kb + j, heads + h)),
                  pl.BlockSpec((tk, LANES), lambda h, i, j: (kb + j, 2 * heads + h)),
                  pl.BlockSpec((1, LANES), lambda h, i, j: (0, 0))],
        out_specs=pl.BlockSpec((tq, LANES), lambda h, i, j: (i, h)),
        out_shape=jax.ShapeDtypeStruct((n_q, heads * LANES), BF16),
        scratch_shapes=[pltpu.VMEM((2, tq, LANES), BF16), pltpu.VMEM((2, tq, 1), F32),
                        pltpu.VMEM((2, tq, 2 * LANES), F32)],
        compiler_params=_cp("arbitrary", "arbitrary", "arbitrary"),
        name="diff_attn",
    )(lam2, qkv, qkv, qkv, subln_g.reshape(1, LANES))


def _na_bias_tables(rpb, rows):
    wr, wc = NA_WIN_ROWS, NA_WIN_COLS
    heads = rpb.shape[0]
    c = np.arange(GRID_W)[:, None]
    kc = np.arange(GRID_W)[None, :]
    cs = np.clip(c - wc // 2, 0, GRID_W - wc)
    cvalid = (kc >= cs) & (kc < cs + wc)
    onehot = ((kc - c + wc - 1)[None] == np.arange(2 * wc - 1)[:, None, None]) & cvalid[None]
    sel = jnp.asarray(onehot.reshape(2 * wc - 1, GRID_W * GRID_W), F32)
    toe = jnp.dot(rpb.astype(F32).reshape(heads * (2 * wr - 1), 2 * wc - 1), sel,
                  precision=lax.Precision.HIGHEST)
    toe = toe.reshape(heads, 2 * wr - 1, GRID_W, GRID_W) + jnp.asarray(np.where(cvalid, 0.0, NEG), F32)
    toe = jnp.concatenate([toe, jnp.full((heads, 1, GRID_W, GRID_W), NEG, F32)], axis=1)
    r0s = np.array([0, NA_QROWS, rows - NA_QROWS])
    ks = np.clip(r0s - wr // 2, 0, rows - NA_KROWS)
    r = r0s[:, None, None] + np.arange(NA_QROWS)[None, :, None]
    kr = ks[:, None, None] + np.arange(NA_KROWS)[None, None, :]
    rs = np.clip(r - wr // 2, 0, rows - wr)
    idx = np.where((kr >= rs) & (kr < rs + wr), kr - r + wr - 1, 2 * wr - 1)
    blocks = toe[:, idx]
    return blocks.transpose(0, 1, 2, 4, 3, 5).reshape(heads, 3, NA_QROWS * GRID_W, NA_KROWS * GRID_W)


def _na_kernel(q_ref, k_ref, v_ref, b_ref, o_ref, *, n_lat, n_ctx, rows):
    rb = pl.program_id(1)
    n_rb = rows // NA_QROWS
    nt = (((1,), (1,)), ((), ()))
    heads = [slice(hh * LANES, (hh + 1) * LANES) for hh in range(NA_HPS)]

    def ctx_scores(c):
        kc = k_ref[n_lat:n_lat + n_ctx, c]
        return lax.dot_general(q_ref[:, c], kc, nt, preferred_element_type=F32), v_ref[n_lat:n_lat + n_ctx, c]

    @pl.when(rb < n_rb)
    def _():
        ks = jnp.clip(rb * NA_QROWS - NA_WIN_ROWS // 2, 0, rows - NA_KROWS)
        start = pl.multiple_of(ks * GRID_W, GRID_W)
        for hh, c in enumerate(heads):
            s_ctx, vc = ctx_scores(c)
            kw = k_ref[pl.ds(start, NA_KROWS * GRID_W), c]
            vw = v_ref[pl.ds(start, NA_KROWS * GRID_W), c]
            s_loc = lax.dot_general(q_ref[:, c], kw, nt, preferred_element_type=F32) + b_ref[hh, 0]
            m = jnp.maximum(jnp.max(s_loc, axis=1, keepdims=True), jnp.max(s_ctx, axis=1, keepdims=True))
            p_loc = jnp.exp(s_loc - m)
            p_ctx = jnp.exp(s_ctx - m)
            l = jnp.sum(p_loc, axis=1, keepdims=True) + jnp.sum(p_ctx, axis=1, keepdims=True)
            o = (jnp.dot(p_ctx.astype(BF16), vc, preferred_element_type=F32)
                 + jnp.dot(p_loc.astype(BF16), vw, preferred_element_type=F32))
            o_ref[:, c] = (o / l).astype(o_ref.dtype)

    @pl.when(rb == n_rb)
    def _():
        for c in heads:
            s_ctx, vc = ctx_scores(c)
            m = jnp.max(s_ctx, axis=1, keepdims=True)
            p = jnp.exp(s_ctx - m)
            l = jnp.sum(p, axis=1, keepdims=True)
            o = jnp.dot(p.astype(BF16), vc, preferred_element_type=F32)
            o_ref[:, c] = (o / l).astype(o_ref.dtype)


def _na_attn(qkv, bias, n_lat, n_ctx, da_heads, heads):
    t = qkv.shape[0]
    rows = n_lat // GRID_W
    n_rb = rows // NA_QROWS
    tq = NA_QROWS * GRID_W
    assert tq == ROW_TILE and n_ctx == ROW_TILE
    assert heads % NA_HPS == 0 and (3 * da_heads) % NA_HPS == 0
    hg = heads // NA_HPS
    w = NA_HPS * LANES
    q0, k0, v0 = 3 * da_heads // NA_HPS, 3 * da_heads // NA_HPS + hg, 3 * da_heads // NA_HPS + 2 * hg

    def bias_map(h, rb):
        return (h, jnp.where(rb == 0, 0, jnp.where(rb == n_rb - 1, 2, 1)), 0, 0)

    return pl.pallas_call(
        functools.partial(_na_kernel, n_lat=n_lat, n_ctx=n_ctx, rows=rows),
        grid=(hg, n_rb + 1),
        in_specs=[pl.BlockSpec((tq, w), lambda h, rb: (rb, q0 + h)),
                  pl.BlockSpec((t, w), lambda h, rb: (0, k0 + h)),
                  pl.BlockSpec((t, w), lambda h, rb: (0, v0 + h)),
                  pl.BlockSpec((NA_HPS, 1, tq, NA_KROWS * GRID_W), bias_map)],
        out_specs=pl.BlockSpec((tq, w), lambda h, rb: (rb, h)),
        out_shape=jax.ShapeDtypeStruct((t, heads * LANES), BF16),
        compiler_params=_cp("arbitrary", "arbitrary"),
        name="na_attn",
    )(qkv, qkv, qkv, bias)


def _hgrn_prep(q, v, z, lb, rev):
    c = HG_CHUNK
    f = lb + (1.0 - lb) * jax.nn.sigmoid(z)
    kk = 1.0 - f
    g = jnp.log2(f)
    ri = lax.broadcasted_iota(jnp.int32, (c, c), 0)
    ci = lax.broadcasted_iota(jnp.int32, (c, c), 1)
    tri = ((ri <= ci) if rev else (ri >= ci)).astype(BF16)
    g_hi = g.astype(BF16)
    r1 = g - g_hi.astype(F32)
    g_mid = r1.astype(BF16)
    g_lo = (r1 - g_mid.astype(F32)).astype(BF16)
    c2 = jnp.dot(tri, jnp.concatenate([g_hi, g_mid], axis=1), preferred_element_type=F32)
    cum = c2[:, :LANES] + c2[:, LANES:] + jnp.dot(tri, g_lo, preferred_element_type=F32)
    total = jnp.sum(g, axis=0, keepdims=True)
    return dict(cum=cum, total=total, kk=kk, qs=_silu(q) * (LANES ** -0.5), v=v, vb=v.astype(BF16))


def _hgrn_intra(pp, rev):
    c, sub, nsub = HG_CHUNK, HG_SUB, HG_CHUNK // HG_SUB
    nt = (((1,), (1,)), ((), ()))
    cum, kk, qs, v, vb = pp["cum"], pp["kk"], pp["qs"], pp["v"], pp["vb"]
    ngrp = sub // SUBLANES
    ti = lax.broadcasted_iota(jnp.int32, (SUBLANES, LANES), 0)
    ckey = cum - jnp.log2(kk)
    o_blocks = [None] * nsub
    for i in range(nsub):
        lo = (nsub - 1 - i) * sub if rev else i * sub
        hi = lo + sub
        cum_i, q_i = cum[lo:hi], qs[lo:hi]
        og = [None] * ngrp
        if i > 0:
            brow = hi if rev else lo - 1
            bnd = cum[brow:brow + 1]
            elo, ehi = (hi, c) if rev else (0, lo)
            qt = (q_i * jnp.exp2(cum_i - bnd)).astype(BF16)
            kt = (kk[elo:ehi] * jnp.exp2(bnd - cum[elo:ehi])).astype(BF16)
            att = lax.dot_general(qt, kt, nt, preferred_element_type=F32)
            o_i = jnp.dot(att.astype(BF16), vb[elo:ehi], preferred_element_type=F32)
            og = [o_i[gq * SUBLANES:(gq + 1) * SUBLANES] for gq in range(ngrp)]
        zs, dst = [], []
        for s in range(sub):
            gs = s // SUBLANES
            for gq in (range(gs + 1) if rev else range(gs, ngrp)):
                r = slice(gq * SUBLANES, (gq + 1) * SUBLANES)
                dlt = cum_i[r] - ckey[lo + s:lo + s + 1]
                if gq == gs:
                    dlt = jnp.where((ti <= s - r.start) if rev else (ti >= s - r.start), dlt, NEG)
                zs.append(jnp.exp2(dlt) * q_i[r])
                dst.append((gq, s))
        a_all = jnp.sum(jnp.concatenate(zs, axis=0), axis=1, keepdims=True)
        for n, (gq, s) in enumerate(dst):
            term = a_all[n * SUBLANES:(n + 1) * SUBLANES] * v[lo + s:lo + s + 1]
            og[gq] = term if og[gq] is None else og[gq] + term
        o_blocks[(nsub - 1 - i) if rev else i] = jnp.concatenate(og, axis=0)
    return jnp.concatenate(o_blocks, axis=0)


def _hgrn_carry(pp, st):
    cum, total = pp["cum"], pp["total"]
    o_inter = lax.dot_general((pp["qs"] * jnp.exp2(cum)).astype(BF16), st.astype(BF16),
                              (((1,), (1,)), ((), ())), preferred_element_type=F32)
    kdec = (pp["kk"] * jnp.exp2(total - cum)).astype(BF16)
    st_new = st * jnp.exp2(total) + lax.dot_general(pp["vb"], kdec, (((0,), (0,)), ((), ())),
                                                    preferred_element_type=F32)
    return o_inter, st_new


def _hgrn_kernel(*refs, rev, final, n_chunks):
    if final:
        q_ref, v_ref, z_ref, lb_ref, of_ref, gate_ref, gn_ref, o_ref, st_ref = refs
    else:
        q_ref, v_ref, z_ref, lb_ref, o_ref, st_ref = refs

    @pl.when(pl.program_id(1) == 0)
    def _():
        st_ref[...] = jnp.zeros(st_ref.shape, F32)

    order = [(n_chunks - 1 - s) if rev else s for s in range(n_chunks)]
    tiles = [(slice(ch * HG_CHUNK, (ch + 1) * HG_CHUNK), hh, slice(hh * LANES, (hh + 1) * LANES))
             for ch in order for hh in range(HG_HPS)]
    preps = [_hgrn_prep(q_ref[rows, cols], v_ref[rows, cols], z_ref[rows, cols], lb_ref[:, cols], rev)
             for rows, _, cols in tiles]
    intras = [_hgrn_intra(pp, rev) for pp in preps]
    states = [st_ref[hh] for hh in range(HG_HPS)]
    for (rows, hh, cols), pp, o in zip(tiles, preps, intras):
        o_inter, states[hh] = _hgrn_carry(pp, states[hh])
        o = o + o_inter
        if final:
            o = _rms(o + of_ref[rows, cols], gn_ref[...]) * _silu(gate_ref[rows, cols])
        o_ref[rows, cols] = o.astype(o_ref.dtype)
    for hh in range(HG_HPS):
        st_ref[hh] = states[hh]


def _hgrn_dir(p, lb_dir, rev, n_lat, heads, o_fwd=None, gnorm_g=None):
    t = p.shape[0]
    d = heads * LANES
    n_blk = t // ROW_TILE
    last = n_blk - 1
    final = o_fwd is not None

    def tok(j):
        return jnp.where(j == 0, last, (last - j) if rev else (j - 1))

    assert heads % HG_HPS == 0
    hg = heads // HG_HPS
    w = HG_HPS * LANES
    zcol = 3 * hg if rev else 2 * hg
    in_specs = [pl.BlockSpec((ROW_TILE, w), lambda h, j: (tok(j), h)),
                pl.BlockSpec((ROW_TILE, w), lambda h, j: (tok(j), hg + h)),
                pl.BlockSpec((ROW_TILE, w), lambda h, j: (tok(j), zcol + h)),
                pl.BlockSpec((1, w), lambda h, j: (0, h))]
    args = [p, p, p, lb_dir.reshape(1, d)]
    if final:
        in_specs += [pl.BlockSpec((ROW_TILE, w), lambda h, j: (tok(j), h)),
                     pl.BlockSpec((ROW_TILE, w), lambda h, j: (tok(j), 4 * hg + h)),
                     pl.BlockSpec((1, LANES), lambda h, j: (0, 0))]
        args += [o_fwd, p, gnorm_g.reshape(1, LANES)]
    return pl.pallas_call(
        functools.partial(_hgrn_kernel, rev=rev, final=final, n_chunks=ROW_TILE // HG_CHUNK),
        grid=(hg, n_blk),
        in_specs=in_specs,
        out_specs=pl.BlockSpec((ROW_TILE, w), lambda h, j: (tok(j), h)),
        out_shape=jax.ShapeDtypeStruct((t, d), BF16 if final else F32),
        scratch_shapes=[pltpu.VMEM((HG_HPS, LANES, LANES), F32)],
        compiler_params=_cp("arbitrary", "arbitrary"),
        name="hgrn_bwd" if rev else "hgrn_fwd",
    )(*args)


def _new_expert(be_ref):
    b = pl.program_id(0)
    return jnp.logical_or(b == 0, be_ref[b] != be_ref[jnp.maximum(b - 1, 0)])


def _expert_weights(be_ref, nu_ref, ws_ref, nx_ref, w_hbms, wbuf, wsem, wb_refs, layer):
    b = pl.program_id(0)

    def copies(expert, slot):
        return [pltpu.make_async_copy(w.at[layer, expert], wbuf.at[slot, i], wsem.at[slot])
                for i, w in enumerate(w_hbms)]

    @pl.when(b == 0)
    def _():
        for cp in copies(be_ref[0], ws_ref[0]):
            cp.start(priority=1)

    @pl.when(jnp.logical_and(b < nu_ref[0], _new_expert(be_ref)))
    def _():
        slot = ws_ref[b]
        for cp in copies(be_ref[b], slot):
            cp.wait()

        @pl.when(nx_ref[b] >= 0)
        def _():
            for cp in copies(nx_ref[b], 1 - slot):
                cp.start(priority=1)

        for i, wb in enumerate(wb_refs):
            wb[...] = wbuf[slot, i].astype(BF16)


def _moe_up_kernel(be_ref, nu_ref, ws_ref, nx_ref, base_ref, nval_ref, tok_ref, x_hbm, wg_hbm, wu_hbm, o_ref,
                   wgb_ref, wub_ref, xbuf, sem, wbuf, wsem, *, layer):
    b = pl.program_id(0)
    nch = wgb_ref.shape[0] // LANES

    def gather(blk, start):
        base = base_ref[blk]
        _row_gather(lambda r: tok_ref[base + r], x_hbm, xbuf.at[blk % 2], sem.at[blk % 2],
                    nval_ref[blk], start, span=nch)

    @pl.when(b == 0)
    def _():
        xbuf[...] = jnp.zeros(xbuf.shape, xbuf.dtype)
        gather(b, True)

    @pl.when(b + 1 < nu_ref[0])
    def _():
        gather(b + 1, True)

    _expert_weights(be_ref, nu_ref, ws_ref, nx_ref, (wg_hbm, wu_hbm), wbuf, wsem, (wgb_ref, wub_ref), layer)

    @pl.when(b < nu_ref[0])
    def _():
        gather(b, False)
        xb = xbuf.at[b % 2]
        x = jnp.concatenate([xb[pl.ds(j, MOE_BM, stride=nch), :] for j in range(nch)], axis=1).astype(BF16)
        hg = jnp.dot(x, wgb_ref[...], preferred_element_type=F32)
        hu = jnp.dot(x, wub_ref[...], preferred_element_type=F32)
        o_ref[...] = (_silu(hg) * hu).astype(o_ref.dtype)

    @pl.when(b >= nu_ref[0])
    def _():
        o_ref[...] = jnp.zeros(o_ref.shape, o_ref.dtype)


def _moe_down_kernel(be_ref, nu_ref, ws_ref, nx_ref, h_ref, wd_hbm, o_ref, wdb_ref, wbuf, wsem, *, layer):
    b = pl.program_id(0)
    _expert_weights(be_ref, nu_ref, ws_ref, nx_ref, (wd_hbm,), wbuf, wsem, (wdb_ref,), layer)

    @pl.when(b < nu_ref[0])
    def _():
        o_ref[...] = jnp.dot(h_ref[...], wdb_ref[...], preferred_element_type=F32)

    @pl.when(b >= nu_ref[0])
    def _():
        o_ref[...] = jnp.zeros(o_ref.shape, o_ref.dtype)


def _moe_experts(m, tok_sorted, blk_expert, blk_base, blk_nval, n_used, blk_wslot, blk_next, w_gate, w_up,
                 w_down, layer):
    d = w_gate.shape[2]
    n_blk = blk_expert.shape[0]
    n_slot = n_blk * MOE_BM
    de = w_gate.shape[3]
    hbm = pl.BlockSpec(memory_space=pl.ANY)
    hid = pl.pallas_call(
        functools.partial(_moe_up_kernel, layer=layer),
        grid_spec=pltpu.PrefetchScalarGridSpec(
            num_scalar_prefetch=7, grid=(n_blk,),
            in_specs=[hbm, hbm, hbm],
            out_specs=pl.BlockSpec((MOE_BM, de), lambda b, *_: (b, 0)),
            scratch_shapes=[pltpu.VMEM((d, de), BF16), pltpu.VMEM((d, de), BF16),
                            pltpu.VMEM((2, MOE_BM * (d // LANES), LANES), F32), pltpu.SemaphoreType.DMA((2,)),
                            pltpu.VMEM((2, 2, d, de), F32), pltpu.SemaphoreType.DMA((2,))]),
        out_shape=jax.ShapeDtypeStruct((n_slot, de), BF16),
        compiler_params=_cp("arbitrary"),
        name="moe_up",
    )(blk_expert, n_used, blk_wslot, blk_next, blk_base, blk_nval, tok_sorted, m, w_gate, w_up)
    return pl.pallas_call(
        functools.partial(_moe_down_kernel, layer=layer),
        grid_spec=pltpu.PrefetchScalarGridSpec(
            num_scalar_prefetch=4, grid=(n_blk,),
            in_specs=[pl.BlockSpec((MOE_BM, de), lambda b, *_: (b, 0)), hbm],
            out_specs=pl.BlockSpec((MOE_BM, d), lambda b, *_: (b, 0)),
            scratch_shapes=[pltpu.VMEM((de, d), BF16), pltpu.VMEM((2, 1, de, d), F32),
                            pltpu.SemaphoreType.DMA((2,))]),
        out_shape=jax.ShapeDtypeStruct((n_slot, d), F32),
        compiler_params=_cp("arbitrary"),
        name="moe_down",
    )(blk_expert, n_used, blk_wslot, blk_next, hid, w_down)


def _route(logits):
    n_exp = N_GROUPS * EXPERTS_PER_GROUP
    t = logits.shape[0]
    g_logits = logits[:, :N_GROUPS]
    g_prob = jax.nn.softmax(g_logits, axis=-1)
    g_sel = jnp.argmax(g_logits, axis=-1).astype(jnp.int32)
    g_w = jnp.take_along_axis(g_prob, g_sel[:, None], axis=1)[:, 0]
    e_logits = logits[:, N_GROUPS:N_GROUPS + n_exp].reshape(t, N_GROUPS, EXPERTS_PER_GROUP)
    e_logits = jnp.take_along_axis(e_logits, g_sel[:, None, None], axis=1)[:, 0]
    top_v, top_i = lax.top_k(e_logits, TOP_K)
    weights = g_w[:, None] * jax.nn.softmax(top_v, axis=-1)
    expert = g_sel[:, None] * EXPERTS_PER_GROUP + top_i.astype(jnp.int32)
    return expert, weights


def _dispatch(expert, weights):
    n_exp = N_GROUPS * EXPERTS_PER_GROUP
    t = expert.shape[0]
    n_as = t * TOP_K
    eid = expert.reshape(-1)
    aidx = jnp.arange(n_as, dtype=jnp.int32)
    eid_s, order = lax.sort((eid, aidx), num_keys=1, is_stable=True)
    onehot = eid_s[:, None] == jnp.arange(n_exp, dtype=jnp.int32)[None, :]
    counts = jnp.sum(onehot, axis=0, dtype=jnp.int32)
    padded = (counts + MOE_BM - 1) // MOE_BM * MOE_BM
    start = jnp.cumsum(counts) - counts
    pend = jnp.cumsum(padded)
    pstart = pend - padded
    dest = aidx + jnp.sum(jnp.where(onehot, (pstart - start)[None, :], 0), axis=1)
    n_blk = -(-n_as // MOE_BM) + n_exp
    blk_expert = jnp.searchsorted(pend, jnp.arange(n_blk, dtype=jnp.int32) * MOE_BM, side='right')
    blk_expert = jnp.minimum(blk_expert, n_exp - 1).astype(jnp.int32)
    n_used = (pend[-1] // MOE_BM).astype(jnp.int32).reshape(1)
    blk_first = jnp.arange(n_blk, dtype=jnp.int32) * MOE_BM - pstart[blk_expert]
    blk_base = jnp.clip(start[blk_expert] + blk_first, 0, n_as - 1).astype(jnp.int32)
    blk_nval = jnp.clip(counts[blk_expert] - blk_first, 0, MOE_BM).astype(jnp.int32)
    blk_base = jnp.minimum(blk_base, n_as - jnp.maximum(blk_nval, 1))
    _, slot_of = lax.sort((order, dest), num_keys=1)
    slot_idx = slot_of.reshape(t, TOP_K).T.reshape(-1)
    used = counts > 0
    ids = jnp.arange(n_exp, dtype=jnp.int32)
    wslot_e = (jnp.cumsum(used.astype(jnp.int32)) - 1) % 2
    later = jnp.where(used[None, :] & (ids[None, :] > ids[:, None]), ids[None, :], n_exp)
    next_e = jnp.min(later, axis=1)
    next_e = jnp.where(next_e == n_exp, -1, next_e).astype(jnp.int32)
    blk_wslot = jnp.maximum(wslot_e, 0)[blk_expert].astype(jnp.int32)
    blk_next = next_e[blk_expert]
    return order // TOP_K, blk_expert, blk_base, blk_nval, n_used, blk_wslot, blk_next, slot_idx


def _moe(m, logits, w_gate, w_up, w_down, layer):
    expert, weights = _route(logits)
    tok_sorted, blk_expert, blk_base, blk_nval, n_used, blk_wslot, blk_next, slot_idx = _dispatch(expert, weights)
    ys = _moe_experts(m, tok_sorted, blk_expert, blk_base, blk_nval, n_used, blk_wslot, blk_next,
                      w_gate, w_up, w_down, layer)
    return ys, slot_idx, weights.T[:, :, None]


def _diff_lambda(lam_p, layer_idx):
    lam_init = 0.8 - 0.6 * math.exp(-0.3 * layer_idx)
    lf = lam_p.astype(F32)
    lam = jnp.exp(jnp.sum(lf[0] * lf[1])) - jnp.exp(jnp.sum(lf[2] * lf[3])) + lam_init
    return jnp.stack([lam, jnp.asarray(1.0 - lam_init, F32)]).astype(F32)


def kernel(x, c, ctx, c_ctx, w_mod, b_mod, norm1_g, norm2_g, att_w_in, att_w_out, att_lambda, att_subln_g,
           att_rpb, rec_w_in, rec_w_out, rec_lb_logits, rec_gnorm_g, moe_w_group, moe_b_group, moe_w_router,
           moe_b_router, moe_w_gate, moe_w_up, moe_w_down, final_norm_g):
    b_, n_lat, d = x.shape
    n_ctx = ctx.shape[1]
    assert b_ == 1 and n_ctx == ROW_TILE and n_lat % ROW_TILE == 0 and d % LANES == 0
    depth = w_mod.shape[0]
    daw, naw = DA_HEADS * LANES, NA_HEADS * LANES
    n_exp = N_GROUPS * EXPERTS_PER_GROUP

    lbp = jax.nn.softmax(rec_lb_logits.astype(F32), axis=0)
    lbs = jnp.cumsum(lbp, axis=0) - lbp[0:1]

    cvec = jnp.zeros((8, d), F32).at[0].set(c[0]).at[1].set(c_ctx)
    mods = _modulation(cvec, w_mod, b_mod)
    cos, sin = _rope_tables(n_lat, n_ctx)

    h = jnp.concatenate([x[0], ctx[0]], axis=0)
    a = _ln_mod(h, norm1_g[0], mods[0], 0, 1, n_lat)
    out = None
    for l in range(depth):
        j = l // 2
        mod = mods[l]
        if l % 2 == 0:
            qkv = _attn_in_proj(a, att_w_in, j, cos, sin, daw, naw)
            lam2 = _diff_lambda(att_lambda[j], l)
            da_lat = _diff_attn(qkv, lam2, att_subln_g[j], 0, n_lat, 0, n_lat + n_ctx, DA_HEADS)
            da_ctx = _diff_attn(qkv, lam2, att_subln_g[j], n_lat, n_ctx, n_lat, n_ctx, DA_HEADS)
            bias = _na_bias_tables(att_rpb[j], n_lat // GRID_W)
            na = _na_attn(qkv, bias, n_lat, n_ctx, DA_HEADS, NA_HEADS)
            y = _matmul(jnp.concatenate([da_lat, da_ctx], axis=0), att_w_out, j, F32, a2=na)
        else:
            p = _matmul(a, rec_w_in, j, F32)
            o_f = _hgrn_dir(p, lbs[l, 0], False, n_lat, HG_HEADS)
            o = _hgrn_dir(p, lbs[l, 1], True, n_lat, HG_HEADS, o_fwd=o_f, gnorm_g=rec_gnorm_g[j])
            y = _matmul(o, rec_w_out, j, F32)
        w_route = jnp.zeros((d, LANES), F32).at[:, :N_GROUPS].set(moe_w_group[l])
        w_route = w_route.at[:, N_GROUPS:N_GROUPS + n_exp].set(moe_w_router[l])
        b_route = jnp.zeros((1, LANES), F32).at[0, :N_GROUPS].set(moe_b_group[l])
        b_route = b_route.at[0, N_GROUPS:N_GROUPS + n_exp].set(moe_b_router[l])
        h, m, logits = _res_ln_mod_route(h, y, norm2_g[l], mod, 2, 3, 4, w_route, b_route, n_lat)
        ys, slot_idx, wt = _moe(m, logits, moe_w_gate, moe_w_up, moe_w_down, l)
        if l + 1 < depth:
            h, a = _res_ln_mod(h, ys, slot_idx, wt, norm1_g[l + 1], mod, 5, mods[l + 1], 0, 1, n_lat)
        else:
            out = _res_final(h, ys, slot_idx, wt, final_norm_g, mod, 5, n_lat)
    return out[None]
```

```python
import functools
import math

import numpy as np
import jax
import jax.numpy as jnp
from jax import lax
from jax.experimental import pallas as pl
from jax.experimental.pallas import tpu as pltpu

F32 = jnp.float32
BF16 = jnp.bfloat16

GRID_W = 64
EPS = 1e-6
DA_HEADS = 8
DA_QK_DIM = 64
NA_HEADS = 8
NA_WIN_ROWS = 8
NA_WIN_COLS = 16
ROPE_THETA = 10000.0
HG_HEADS = 16
N_GROUPS = 4
EXPERTS_PER_GROUP = 8
TOP_K = 2

LANES = 128
SUBLANES = 8
ROW_TILE = 256
HG_CHUNK = 64
HG_SUB = 32
HG_HPS = 2
NA_QROWS = 4
NA_KROWS = 12
NA_HPS = 4
MOE_BM = 256
NEG = -1e30
VMEM_LIMIT = 56 << 20


def _cp(*sem):
    return pltpu.CompilerParams(dimension_semantics=sem, vmem_limit_bytes=VMEM_LIMIT)


def _pick(n, cands):
    for c in cands:
        if n % c == 0:
            return c
    raise ValueError(f"no tile for {n} in {cands}")


def _silu(x):
    return x * jax.nn.sigmoid(x)


def _rms(x, g):
    return x * lax.rsqrt(jnp.mean(x * x, axis=-1, keepdims=True) + EPS) * g


def _mod_kernel(c_ref, w_ref, b_ref, o_ref):
    s = _silu(c_ref[...]).astype(BF16)
    o_ref[0] = jnp.dot(s, w_ref[0].astype(BF16), preferred_element_type=F32) + b_ref[0]


def _modulation(cvec, w_mod, b_mod):
    depth, d, n = w_mod.shape
    tn = _pick(n, (1024, 512, 256, 128))
    return pl.pallas_call(
        _mod_kernel,
        grid=(depth, n // tn),
        in_specs=[pl.BlockSpec((8, d), lambda l, j: (0, 0)),
                  pl.BlockSpec((1, d, tn), lambda l, j: (l, 0, j)),
                  pl.BlockSpec((1, 1, tn), lambda l, j: (l, 0, j))],
        out_specs=pl.BlockSpec((1, 8, tn), lambda l, j: (l, 0, j)),
        out_shape=jax.ShapeDtypeStruct((depth, 8, n), F32),
        compiler_params=_cp("arbitrary", "arbitrary"),
        name="modulation",
    )(cvec, w_mod, b_mod.reshape(depth, 1, n))


def _mod_row(ref, n_lat_tiles):
    r = (pl.program_id(0) >= n_lat_tiles).astype(jnp.int32)
    return ref[pl.ds(r, 1), :]


def _ln_mod_kernel(h_ref, g_ref, sh_ref, sc_ref, a_ref, *, n_lat_tiles):
    y = _rms(h_ref[...], g_ref[...])
    a = y * (1.0 + _mod_row(sc_ref, n_lat_tiles)) + _mod_row(sh_ref, n_lat_tiles)
    a_ref[...] = a.astype(a_ref.dtype)


def _row_gather(row_of, src_hbm, dst, sem, n_rows, start, span=1, two_queues=False):
    def copy(r):
        return pltpu.make_async_copy(src_hbm.at[pl.ds(pl.multiple_of(row_of(r) * span, span), span)],
                                     dst.at[pl.ds(pl.multiple_of(r * span, span), span)], sem)

    def body(r, carry):
        if start:
            copy(r).start()
        else:
            copy(r).wait()
        return carry

    def body_two_queues(g, carry):
        for j in range(8):
            copy(g * 8 + j).start(priority=j % 2)
        return carry

    if start and two_queues:
        assert isinstance(n_rows, int) and n_rows % 8 == 0
        lax.fori_loop(0, n_rows // 8, body_two_queues, 0)
    elif isinstance(n_rows, int):
        lax.fori_loop(0, n_rows, body, 0, unroll=8)
    else:
        lax.fori_loop(0, n_rows, body, 0)


def _combined_expert_rows(idx_ref, wt_ref, ys_hbm, ybuf, sem, t):
    i = pl.program_id(0)

    def gather(tile, start):
        slot = tile % 2
        for k in range(TOP_K):
            base = k * t + tile * ROW_TILE
            _row_gather(lambda r: idx_ref[base + r], ys_hbm, ybuf.at[slot, k], sem.at[slot], ROW_TILE, start,
                        two_queues=True)

    @pl.when(i == 0)
    def _():
        gather(i, True)

    @pl.when(i + 1 < pl.num_programs(0))
    def _():
        gather(i + 1, True)

    gather(i, False)
    y = wt_ref[0] * ybuf[i % 2, 0]
    for k in range(1, TOP_K):
        y = y + wt_ref[k] * ybuf[i % 2, k]
    return y


def _res_ln_mod_kernel(idx_ref, h_ref, wt_ref, ys_hbm, gt_ref, g_ref, sh_ref, sc_ref, ho_ref, a_ref, ybuf, sem,
                       *, n_lat_tiles, t):
    y = _combined_expert_rows(idx_ref, wt_ref, ys_hbm, ybuf, sem, t)
    h = h_ref[...] + _mod_row(gt_ref, n_lat_tiles) * y
    ho_ref[...] = h
    a = _rms(h, g_ref[...]) * (1.0 + _mod_row(sc_ref, n_lat_tiles)) + _mod_row(sh_ref, n_lat_tiles)
    a_ref[...] = a.astype(a_ref.dtype)


def _res_ln_mod_route_kernel(h_ref, y_ref, gt_ref, g_ref, sh_ref, sc_ref, wr_ref, br_ref,
                             ho_ref, a_ref, lg_ref, *, n_lat_tiles):
    h = h_ref[...] + _mod_row(gt_ref, n_lat_tiles) * y_ref[...]
    ho_ref[...] = h
    a = _rms(h, g_ref[...]) * (1.0 + _mod_row(sc_ref, n_lat_tiles)) + _mod_row(sh_ref, n_lat_tiles)
    nch = a.shape[1] // LANES
    for j in range(nch):
        a_ref[pl.ds(j, ROW_TILE, stride=nch), :] = a[:, j * LANES:(j + 1) * LANES]
    lg_ref[...] = jnp.dot(a, wr_ref[...], preferred_element_type=F32,
                          precision=lax.Precision.HIGHEST) + br_ref[...]


def _res_final_kernel(idx_ref, h_ref, wt_ref, ys_hbm, gt_ref, g_ref, o_ref, ybuf, sem, *, t):
    y = _combined_expert_rows(idx_ref, wt_ref, ys_hbm, ybuf, sem, t)
    h = h_ref[...] + gt_ref[pl.ds(0, 1), :] * y
    o_ref[...] = _rms(h, g_ref[...])


def _row_spec(d):
    return pl.BlockSpec((ROW_TILE, d), lambda i, *_: (i, 0))


def _vec_spec(d):
    return pl.BlockSpec((1, d), lambda i, *_: (0, 0))


def _mod_spec(d, k):
    return pl.BlockSpec((8, d), lambda i, *_: (0, k))


def _combine_scratch(d):
    return [pltpu.VMEM((2, TOP_K, ROW_TILE, d), F32), pltpu.SemaphoreType.DMA((2,))]


def _wt_spec():
    return pl.BlockSpec((TOP_K, ROW_TILE, 1), lambda i, *_: (0, i, 0))


def _ln_mod(h, g, mod, k_shift, k_scale, n_lat):
    t, d = h.shape
    return pl.pallas_call(
        functools.partial(_ln_mod_kernel, n_lat_tiles=n_lat // ROW_TILE),
        grid=(t // ROW_TILE,),
        in_specs=[_row_spec(d), _vec_spec(d), _mod_spec(d, k_shift), _mod_spec(d, k_scale)],
        out_specs=_row_spec(d),
        out_shape=jax.ShapeDtypeStruct((t, d), BF16),
        compiler_params=_cp("arbitrary"),
        name="ln_mod",
    )(h, g.reshape(1, d), mod, mod)


def _res_ln_mod(h, ys, slot_idx, wt, g, mod_gate, k_gate, mod_next, k_shift, k_scale, n_lat):
    t, d = h.shape
    return pl.pallas_call(
        functools.partial(_res_ln_mod_kernel, n_lat_tiles=n_lat // ROW_TILE, t=t),
        grid_spec=pltpu.PrefetchScalarGridSpec(
            num_scalar_prefetch=1, grid=(t // ROW_TILE,),
            in_specs=[_row_spec(d), _wt_spec(), pl.BlockSpec(memory_space=pl.ANY), _mod_spec(d, k_gate),
                      _vec_spec(d), _mod_spec(d, k_shift), _mod_spec(d, k_scale)],
            out_specs=[_row_spec(d), _row_spec(d)],
            scratch_shapes=_combine_scratch(d)),
        out_shape=[jax.ShapeDtypeStruct((t, d), F32), jax.ShapeDtypeStruct((t, d), BF16)],
        compiler_params=_cp("arbitrary"),
        name="res_ln_mod",
    )(slot_idx, h, wt, ys, mod_gate, g.reshape(1, d), mod_next, mod_next)


def _res_ln_mod_route(h, y, g, mod, k_gate, k_shift, k_scale, w_route, b_route, n_lat):
    t, d = h.shape
    return pl.pallas_call(
        functools.partial(_res_ln_mod_route_kernel, n_lat_tiles=n_lat // ROW_TILE),
        grid=(t // ROW_TILE,),
        in_specs=[_row_spec(d), _row_spec(d), _mod_spec(d, k_gate), _vec_spec(d),
                  _mod_spec(d, k_shift), _mod_spec(d, k_scale),
                  pl.BlockSpec((d, LANES), lambda i: (0, 0)), _vec_spec(LANES)],
        out_specs=[_row_spec(d), pl.BlockSpec((ROW_TILE * (d // LANES), LANES), lambda i: (i, 0)),
                   _row_spec(LANES)],
        out_shape=[jax.ShapeDtypeStruct((t, d), F32), jax.ShapeDtypeStruct((t * (d // LANES), LANES), F32),
                   jax.ShapeDtypeStruct((t, LANES), F32)],
        compiler_params=_cp("arbitrary"),
        name="res_ln_mod_route",
    )(h, y, mod, g.reshape(1, d), mod, mod, w_route, b_route)


def _res_final(h, ys, slot_idx, wt, g, mod, k_gate, n_lat):
    t, d = h.shape
    return pl.pallas_call(
        functools.partial(_res_final_kernel, t=t),
        grid_spec=pltpu.PrefetchScalarGridSpec(
            num_scalar_prefetch=1, grid=(n_lat // ROW_TILE,),
            in_specs=[_row_spec(d), _wt_spec(), pl.BlockSpec(memory_space=pl.ANY), _mod_spec(d, k_gate),
                      _vec_spec(d)],
            out_specs=_row_spec(d),
            scratch_shapes=_combine_scratch(d)),
        out_shape=jax.ShapeDtypeStruct((n_lat, d), F32),
        compiler_params=_cp("arbitrary"),
        name="res_final",
    )(slot_idx, h, wt, ys, mod, g.reshape(1, d))


def _matmul_kernel(a_ref, w_ref, o_ref, wb_ref):
    @pl.when(pl.program_id(1) == 0)
    def _():
        wb_ref[...] = w_ref[0].astype(BF16)

    o_ref[...] = jnp.dot(a_ref[...], wb_ref[...], preferred_element_type=F32).astype(o_ref.dtype)


def _matmul2_kernel(a1_ref, a2_ref, w_ref, o_ref, wb_ref):
    @pl.when(pl.program_id(1) == 0)
    def _():
        wb_ref[...] = w_ref[0].astype(BF16)

    k1 = a1_ref.shape[1]
    o_ref[...] = (jnp.dot(a1_ref[...], wb_ref[:k1, :], preferred_element_type=F32)
                  + jnp.dot(a2_ref[...], wb_ref[k1:, :], preferred_element_type=F32)).astype(o_ref.dtype)


def _matmul(a, w_stack, layer, out_dtype, a2=None):
    m = a.shape[0]
    ops = [a] if a2 is None else [a, a2]
    k = w_stack.shape[1]
    n = w_stack.shape[2]
    assert sum(x.shape[1] for x in ops) == k
    tm = _pick(m, (768, 640, 512, 256))
    tn = _pick(n, (1024, 512, 256, 128))
    return pl.pallas_call(
        _matmul_kernel if a2 is None else _matmul2_kernel,
        grid=(n // tn, m // tm),
        in_specs=[pl.BlockSpec((tm, x.shape[1]), lambda j, i: (i, 0)) for x in ops]
        + [pl.BlockSpec((1, k, tn), lambda j, i: (layer, 0, j))],
        out_specs=pl.BlockSpec((tm, tn), lambda j, i: (i, j)),
        out_shape=jax.ShapeDtypeStruct((m, n), out_dtype),
        scratch_shapes=[pltpu.VMEM((k, tn), BF16)],
        compiler_params=_cp("arbitrary", "arbitrary"),
        name="matmul",
    )(*ops, w_stack)


def _attn_in_kernel(a_ref, w_ref, cos_ref, sin_ref, o_ref, wb_ref, *, daw, naw, sa, sn):
    @pl.when(pl.program_id(1) == 0)
    def _():
        wb_ref[...] = w_ref[0].astype(BF16)

    tn = o_ref.shape[1]
    c0 = pl.program_id(0) * tn
    p = jnp.dot(a_ref[...], wb_ref[...], preferred_element_type=F32)

    @pl.when(c0 < 2 * daw)
    def _():
        cos = cos_ref[...]
        sin = sin_ref[...]
        lane = lax.broadcasted_iota(jnp.int32, cos.shape, 1)
        first_half = (lane % (DA_QK_DIM // 2)) < (DA_QK_DIM // 4)
        quarter = DA_QK_DIM // 4
        scale = jnp.where(c0 < daw, sa, 1.0).astype(F32)
        for g in range(tn // LANES):
            x = p[:, g * LANES:(g + 1) * LANES]
            partner = jnp.where(first_half, pltpu.roll(x, LANES - quarter, 1), pltpu.roll(x, quarter, 1))
            o_ref[:, g * LANES:(g + 1) * LANES] = ((x * cos + partner * sin) * scale).astype(o_ref.dtype)

    @pl.when(c0 >= 2 * daw)
    def _():
        scale = jnp.where(jnp.logical_and(c0 >= 3 * daw, c0 < 3 * daw + naw), sn, 1.0).astype(F32)
        o_ref[...] = (p * scale).astype(o_ref.dtype)


def _rope_tables(n_lat, n_ctx):
    t = jnp.arange(n_lat, dtype=jnp.int32)
    rows = (t // GRID_W).astype(F32)
    cols = (t % GRID_W).astype(F32)
    n_freq = DA_QK_DIM // 4
    inv_freq = ROPE_THETA ** (-jnp.arange(n_freq, dtype=F32) / n_freq)
    ang_r = rows[:, None] * inv_freq
    ang_c = cols[:, None] * inv_freq
    ang = jnp.concatenate([ang_r, ang_r, ang_c, ang_c], axis=1)
    sign = jnp.concatenate([-jnp.ones((n_freq,), F32), jnp.ones((n_freq,), F32)] * 2)
    cos = jnp.cos(ang)
    sin = jnp.sin(ang) * sign
    reps = LANES // DA_QK_DIM
    cos = jnp.concatenate([jnp.tile(cos, (1, reps)), jnp.ones((n_ctx, LANES), F32)], axis=0)
    sin = jnp.concatenate([jnp.tile(sin, (1, reps)), jnp.zeros((n_ctx, LANES), F32)], axis=0)
    return cos, sin


def _attn_in_proj(a, w_stack, layer, cos, sin, daw, naw):
    m, k = a.shape
    n = w_stack.shape[2]
    tm = _pick(m, (768, 640, 512, 256))
    tn = _pick(math.gcd(daw, naw), (1024, 512, 256, 128))
    kern = functools.partial(_attn_in_kernel, daw=daw, naw=naw, sa=DA_QK_DIM ** -0.5 * math.log2(math.e),
                             sn=LANES ** -0.5)
    return pl.pallas_call(
        kern,
        grid=(n // tn, m // tm),
        in_specs=[pl.BlockSpec((tm, k), lambda j, i: (i, 0)),
                  pl.BlockSpec((1, k, tn), lambda j, i: (layer, 0, j)),
                  pl.BlockSpec((tm, LANES), lambda j, i: (i, 0)),
                  pl.BlockSpec((tm, LANES), lambda j, i: (i, 0))],
        out_specs=pl.BlockSpec((tm, tn), lambda j, i: (i, j)),
        out_shape=jax.ShapeDtypeStruct((m, n), BF16),
        scratch_shapes=[pltpu.VMEM((k, tn), BF16)],
        compiler_params=_cp("arbitrary", "arbitrary"),
        name="attn_in_proj",
    )(a, w_stack, cos, sin)


def _diff_attn_kernel(lam_ref, q_ref, k_ref, v_ref, g_ref, o_ref, qm_ref, m_ref, acc_ref, *, nk, rg):
    kv = pl.program_id(2)
    tq = q_ref.shape[0]

    @pl.when(kv == 0)
    def _():
        q = q_ref[...]
        lane = lax.broadcasted_iota(jnp.int32, q.shape, 1)
        zero = jnp.zeros_like(q)
        qm_ref[0] = jnp.where(lane < DA_QK_DIM, q, zero)
        qm_ref[1] = jnp.where(lane >= DA_QK_DIM, q, zero)
        m_ref[...] = jnp.full(m_ref.shape, NEG, F32)
        acc_ref[...] = jnp.zeros(acc_ref.shape, F32)

    k = k_ref[...]
    v = v_ref[...]
    vext = jnp.concatenate([v, jnp.ones(v.shape, v.dtype)], axis=1)
    for r0 in range(0, tq, rg):
        for mi in range(2):
            s = lax.dot_general(qm_ref[mi, r0:r0 + rg], k, (((1,), (1,)), ((), ())), preferred_element_type=F32)
            m_prev = m_ref[mi, r0:r0 + rg]
            m_new = jnp.maximum(m_prev, jnp.max(s, axis=1, keepdims=True))
            p = jnp.exp2(s - m_new)
            acc_ref[mi, r0:r0 + rg] = (jnp.exp2(m_prev - m_new) * acc_ref[mi, r0:r0 + rg]
                                       + jnp.dot(p.astype(BF16), vext, preferred_element_type=F32))
            m_ref[mi, r0:r0 + rg] = m_new

    @pl.when(kv == nk - 1)
    def _():
        a0 = acc_ref[0]
        a1 = acc_ref[1]
        o = a0[:, :LANES] / a0[:, LANES:LANES + 1] - lam_ref[0] * (a1[:, :LANES] / a1[:, LANES:LANES + 1])
        o_ref[...] = (_rms(o, g_ref[...]) * lam_ref[1]).astype(o_ref.dtype)


DA_TQ, DA_TK, DA_RG = 1024, 2816, 256


def _diff_attn(qkv, lam2, subln_g, q_row0, n_q, k_row0, n_k, heads):
    tq = _pick(n_q, (DA_TQ, 1024, 512, 256))
    tk = _pick(n_k, (DA_TK, 1408, 1280, 1024, 768, 512, 256))
    assert q_row0 % tq == 0 and k_row0 % tk == 0
    nq, nk = n_q // tq, n_k // tk
    qb, kb = q_row0 // tq, k_row0 // tk
    return pl.pallas_call(
        functools.partial(_diff_attn_kernel, nk=nk, rg=min(DA_RG, tq)),
        grid=(heads, nq, nk),
        in_specs=[pl.BlockSpec(memory_space=pltpu.SMEM),
                  pl.BlockSpec((tq, LANES), lambda h, i, j: (qb + i, h)),
                  pl.BlockSpec((tk, LANES), lambda h, i, j: (kb + j, heads + h)),
                  pl.BlockSpec((tk, LANES), lambda h, i, j: (kb + j, 2 * heads + h)),
                  pl.BlockSpec((1, LANES), lambda h, i, j: (0, 0))],
        out_specs=pl.BlockSpec((tq, LANES), lambda h, i, j: (i, h)),
        out_shape=jax.ShapeDtypeStruct((n_q, heads * LANES), BF16),
        scratch_shapes=[pltpu.VMEM((2, tq, LANES), BF16), pltpu.VMEM((2, tq, 1), F32),
                        pltpu.VMEM((2, tq, 2 * LANES), F32)],
        compiler_params=_cp("arbitrary", "arbitrary", "arbitrary"),
        name="diff_attn",
    )(lam2, qkv, qkv, qkv, subln_g.reshape(1, LANES))


def _na_bias_tables(rpb, rows):
    wr, wc = NA_WIN_ROWS, NA_WIN_COLS
    heads = rpb.shape[0]
    c = np.arange(GRID_W)[:, None]
    kc = np.arange(GRID_W)[None, :]
    cs = np.clip(c - wc // 2, 0, GRID_W - wc)
    cvalid = (kc >= cs) & (kc < cs + wc)
    onehot = ((kc - c + wc - 1)[None] == np.arange(2 * wc - 1)[:, None, None]) & cvalid[None]
    sel = jnp.asarray(onehot.reshape(2 * wc - 1, GRID_W * GRID_W), F32)
    toe = jnp.dot(rpb.astype(F32).reshape(heads * (2 * wr - 1), 2 * wc - 1), sel,
                  precision=lax.Precision.HIGHEST)
    toe = toe.reshape(heads, 2 * wr - 1, GRID_W, GRID_W) + jnp.asarray(np.where(cvalid, 0.0, NEG), F32)
    toe = jnp.concatenate([toe, jnp.full((heads, 1, GRID_W, GRID_W), NEG, F32)], axis=1)
    r0s = np.array([0, NA_QROWS, rows - NA_QROWS])
    ks = np.clip(r0s - wr // 2, 0, rows - NA_KROWS)
    r = r0s[:, None, None] + np.arange(NA_QROWS)[None, :, None]
    kr = ks[:, None, None] + np.arange(NA_KROWS)[None, None, :]
    rs = np.clip(r - wr // 2, 0, rows - wr)
    idx = np.where((kr >= rs) & (kr < rs + wr), kr - r + wr - 1, 2 * wr - 1)
    blocks = toe[:, idx]
    return blocks.transpose(0, 1, 2, 4, 3, 5).reshape(heads, 3, NA_QROWS * GRID_W, NA_KROWS * GRID_W)


def _na_kernel(q_ref, k_ref, v_ref, b_ref, o_ref, *, n_lat, n_ctx, rows):
    rb = pl.program_id(1)
    n_rb = rows // NA_QROWS
    nt = (((1,), (1,)), ((), ()))
    heads = [slice(hh * LANES, (hh + 1) * LANES) for hh in range(NA_HPS)]

    def ctx_scores(c):
        kc = k_ref[n_lat:n_lat + n_ctx, c]
        return lax.dot_general(q_ref[:, c], kc, nt, preferred_element_type=F32), v_ref[n_lat:n_lat + n_ctx, c]

    @pl.when(rb < n_rb)
    def _():
        ks = jnp.clip(rb * NA_QROWS - NA_WIN_ROWS // 2, 0, rows - NA_KROWS)
        start = pl.multiple_of(ks * GRID_W, GRID_W)
        for hh, c in enumerate(heads):
            s_ctx, vc = ctx_scores(c)
            kw = k_ref[pl.ds(start, NA_KROWS * GRID_W), c]
            vw = v_ref[pl.ds(start, NA_KROWS * GRID_W), c]
            s_loc = lax.dot_general(q_ref[:, c], kw, nt, preferred_element_type=F32) + b_ref[hh, 0]
            m = jnp.maximum(jnp.max(s_loc, axis=1, keepdims=True), jnp.max(s_ctx, axis=1, keepdims=True))
            p_loc = jnp.exp(s_loc - m)
            p_ctx = jnp.exp(s_ctx - m)
            l = jnp.sum(p_loc, axis=1, keepdims=True) + jnp.sum(p_ctx, axis=1, keepdims=True)
            o = (jnp.dot(p_ctx.astype(BF16), vc, preferred_element_type=F32)
                 + jnp.dot(p_loc.astype(BF16), vw, preferred_element_type=F32))
            o_ref[:, c] = (o / l).astype(o_ref.dtype)

    @pl.when(rb == n_rb)
    def _():
        for c in heads:
            s_ctx, vc = ctx_scores(c)
            m = jnp.max(s_ctx, axis=1, keepdims=True)
            p = jnp.exp(s_ctx - m)
            l = jnp.sum(p, axis=1, keepdims=True)
            o = jnp.dot(p.astype(BF16), vc, preferred_element_type=F32)
            o_ref[:, c] = (o / l).astype(o_ref.dtype)


def _na_attn(qkv, bias, n_lat, n_ctx, da_heads, heads):
    t = qkv.shape[0]
    rows = n_lat // GRID_W
    n_rb = rows // NA_QROWS
    tq = NA_QROWS * GRID_W
    assert tq == ROW_TILE and n_ctx == ROW_TILE
    assert heads % NA_HPS == 0 and (3 * da_heads) % NA_HPS == 0
    hg = heads // NA_HPS
    w = NA_HPS * LANES
    q0, k0, v0 = 3 * da_heads // NA_HPS, 3 * da_heads // NA_HPS + hg, 3 * da_heads // NA_HPS + 2 * hg

    def bias_map(h, rb):
        return (h, jnp.where(rb == 0, 0, jnp.where(rb == n_rb - 1, 2, 1)), 0, 0)

    return pl.pallas_call(
        functools.partial(_na_kernel, n_lat=n_lat, n_ctx=n_ctx, rows=rows),
        grid=(hg, n_rb + 1),
        in_specs=[pl.BlockSpec((tq, w), lambda h, rb: (rb, q0 + h)),
                  pl.BlockSpec((t, w), lambda h, rb: (0, k0 + h)),
                  pl.BlockSpec((t, w), lambda h, rb: (0, v0 + h)),
                  pl.BlockSpec((NA_HPS, 1, tq, NA_KROWS * GRID_W), bias_map)],
        out_specs=pl.BlockSpec((tq, w), lambda h, rb: (rb, h)),
        out_shape=jax.ShapeDtypeStruct((t, heads * LANES), BF16),
        compiler_params=_cp("arbitrary", "arbitrary"),
        name="na_attn",
    )(qkv, qkv, qkv, bias)


def _hgrn_prep(q, v, z, lb, rev):
    c = HG_CHUNK
    f = lb + (1.0 - lb) * jax.nn.sigmoid(z)
    kk = 1.0 - f
    g = jnp.log2(f)
    ri = lax.broadcasted_iota(jnp.int32, (c, c), 0)
    ci = lax.broadcasted_iota(jnp.int32, (c, c), 1)
    tri = ((ri <= ci) if rev else (ri >= ci)).astype(BF16)
    g_hi = g.astype(BF16)
    r1 = g - g_hi.astype(F32)
    g_mid = r1.astype(BF16)
    g_lo = (r1 - g_mid.astype(F32)).astype(BF16)
    c2 = jnp.dot(tri, jnp.concatenate([g_hi, g_mid], axis=1), preferred_element_type=F32)
    cum = c2[:, :LANES] + c2[:, LANES:] + jnp.dot(tri, g_lo, preferred_element_type=F32)
    total = jnp.sum(g, axis=0, keepdims=True)
    return dict(cum=cum, total=total, kk=kk, qs=_silu(q) * (LANES ** -0.5), v=v, vb=v.astype(BF16))


def _hgrn_intra(pp, rev):
    c, sub, nsub = HG_CHUNK, HG_SUB, HG_CHUNK // HG_SUB
    nt = (((1,), (1,)), ((), ()))
    cum, kk, qs, v, vb = pp["cum"], pp["kk"], pp["qs"], pp["v"], pp["vb"]
    ngrp = sub // SUBLANES
    ti = lax.broadcasted_iota(jnp.int32, (SUBLANES, LANES), 0)
    ckey = cum - jnp.log2(kk)
    o_blocks = [None] * nsub
    for i in range(nsub):
        lo = (nsub - 1 - i) * sub if rev else i * sub
        hi = lo + sub
        cum_i, q_i = cum[lo:hi], qs[lo:hi]
        og = [None] * ngrp
        if i > 0:
            brow = hi if rev else lo - 1
            bnd = cum[brow:brow + 1]
            elo, ehi = (hi, c) if rev else (0, lo)
            qt = (q_i * jnp.exp2(cum_i - bnd)).astype(BF16)
            kt = (kk[elo:ehi] * jnp.exp2(bnd - cum[elo:ehi])).astype(BF16)
            att = lax.dot_general(qt, kt, nt, preferred_element_type=F32)
            o_i = jnp.dot(att.astype(BF16), vb[elo:ehi], preferred_element_type=F32)
            og = [o_i[gq * SUBLANES:(gq + 1) * SUBLANES] for gq in range(ngrp)]
        zs, dst = [], []
        for s in range(sub):
            gs = s // SUBLANES
            for gq in (range(gs + 1) if rev else range(gs, ngrp)):
                r = slice(gq * SUBLANES, (gq + 1) * SUBLANES)
                dlt = cum_i[r] - ckey[lo + s:lo + s + 1]
                if gq == gs:
                    dlt = jnp.where((ti <= s - r.start) if rev else (ti >= s - r.start), dlt, NEG)
                zs.append(jnp.exp2(dlt) * q_i[r])
                dst.append((gq, s))
        a_all = jnp.sum(jnp.concatenate(zs, axis=0), axis=1, keepdims=True)
        for n, (gq, s) in enumerate(dst):
            term = a_all[n * SUBLANES:(n + 1) * SUBLANES] * v[lo + s:lo + s + 1]
            og[gq] = term if og[gq] is None else og[gq] + term
        o_blocks[(nsub - 1 - i) if rev else i] = jnp.concatenate(og, axis=0)
    return jnp.concatenate(o_blocks, axis=0)


def _hgrn_carry(pp, st):
    cum, total = pp["cum"], pp["total"]
    o_inter = lax.dot_general((pp["qs"] * jnp.exp2(cum)).astype(BF16), st.astype(BF16),
                              (((1,), (1,)), ((), ())), preferred_element_type=F32)
    kdec = (pp["kk"] * jnp.exp2(total - cum)).astype(BF16)
    st_new = st * jnp.exp2(total) + lax.dot_general(pp["vb"], kdec, (((0,), (0,)), ((), ())),
                                                    preferred_element_type=F32)
    return o_inter, st_new


def _hgrn_kernel(*refs, rev, final, n_chunks):
    if final:
        q_ref, v_ref, z_ref, lb_ref, of_ref, gate_ref, gn_ref, o_ref, st_ref = refs
    else:
        q_ref, v_ref, z_ref, lb_ref, o_ref, st_ref = refs

    @pl.when(pl.program_id(1) == 0)
    def _():
        st_ref[...] = jnp.zeros(st_ref.shape, F32)

    order = [(n_chunks - 1 - s) if rev else s for s in range(n_chunks)]
    tiles = [(slice(ch * HG_CHUNK, (ch + 1) * HG_CHUNK), hh, slice(hh * LANES, (hh + 1) * LANES))
             for ch in order for hh in range(HG_HPS)]
    preps = [_hgrn_prep(q_ref[rows, cols], v_ref[rows, cols], z_ref[rows, cols], lb_ref[:, cols], rev)
             for rows, _, cols in tiles]
    intras = [_hgrn_intra(pp, rev) for pp in preps]
    states = [st_ref[hh] for hh in range(HG_HPS)]
    for (rows, hh, cols), pp, o in zip(tiles, preps, intras):
        o_inter, states[hh] = _hgrn_carry(pp, states[hh])
        o = o + o_inter
        if final:
            o = _rms(o + of_ref[rows, cols], gn_ref[...]) * _silu(gate_ref[rows, cols])
        o_ref[rows, cols] = o.astype(o_ref.dtype)
    for hh in range(HG_HPS):
        st_ref[hh] = states[hh]


def _hgrn_dir(p, lb_dir, rev, n_lat, heads, o_fwd=None, gnorm_g=None):
    t = p.shape[0]
    d = heads * LANES
    n_blk = t // ROW_TILE
    last = n_blk - 1
    final = o_fwd is not None

    def tok(j):
        return jnp.where(j == 0, last, (last - j) if rev else (j - 1))

    assert heads % HG_HPS == 0
    hg = heads // HG_HPS
    w = HG_HPS * LANES
    zcol = 3 * hg if rev else 2 * hg
    in_specs = [pl.BlockSpec((ROW_TILE, w), lambda h, j: (tok(j), h)),
                pl.BlockSpec((ROW_TILE, w), lambda h, j: (tok(j), hg + h)),
                pl.BlockSpec((ROW_TILE, w), lambda h, j: (tok(j), zcol + h)),
                pl.BlockSpec((1, w), lambda h, j: (0, h))]
    args = [p, p, p, lb_dir.reshape(1, d)]
    if final:
        in_specs += [pl.BlockSpec((ROW_TILE, w), lambda h, j: (tok(j), h)),
                     pl.BlockSpec((ROW_TILE, w), lambda h, j: (tok(j), 4 * hg + h)),
                     pl.BlockSpec((1, LANES), lambda h, j: (0, 0))]
        args += [o_fwd, p, gnorm_g.reshape(1, LANES)]
    return pl.pallas_call(
        functools.partial(_hgrn_kernel, rev=rev, final=final, n_chunks=ROW_TILE // HG_CHUNK),
        grid=(hg, n_blk),
        in_specs=in_specs,
        out_specs=pl.BlockSpec((ROW_TILE, w), lambda h, j: (tok(j), h)),
        out_shape=jax.ShapeDtypeStruct((t, d), BF16 if final else F32),
        scratch_shapes=[pltpu.VMEM((HG_HPS, LANES, LANES), F32)],
        compiler_params=_cp("arbitrary", "arbitrary"),
        name="hgrn_bwd" if rev else "hgrn_fwd",
    )(*args)


def _new_expert(be_ref):
    b = pl.program_id(0)
    return jnp.logical_or(b == 0, be_ref[b] != be_ref[jnp.maximum(b - 1, 0)])


def _expert_weights(be_ref, nu_ref, ws_ref, nx_ref, w_hbms, wbuf, wsem, wb_refs, layer):
    b = pl.program_id(0)

    def copies(expert, slot):
        return [pltpu.make_async_copy(w.at[layer, expert], wbuf.at[slot, i], wsem.at[slot])
                for i, w in enumerate(w_hbms)]

    @pl.when(b == 0)
    def _():
        for cp in copies(be_ref[0], ws_ref[0]):
            cp.start(priority=1)

    @pl.when(jnp.logical_and(b < nu_ref[0], _new_expert(be_ref)))
    def _():
        slot = ws_ref[b]
        for cp in copies(be_ref[b], slot):
            cp.wait()

        @pl.when(nx_ref[b] >= 0)
        def _():
            for cp in copies(nx_ref[b], 1 - slot):
                cp.start(priority=1)

        for i, wb in enumerate(wb_refs):
            wb[...] = wbuf[slot, i].astype(BF16)


def _moe_up_kernel(be_ref, nu_ref, ws_ref, nx_ref, base_ref, nval_ref, tok_ref, x_hbm, wg_hbm, wu_hbm, o_ref,
                   wgb_ref, wub_ref, xbuf, sem, wbuf, wsem, *, layer):
    b = pl.program_id(0)
    nch = wgb_ref.shape[0] // LANES

    def gather(blk, start):
        base = base_ref[blk]
        _row_gather(lambda r: tok_ref[base + r], x_hbm, xbuf.at[blk % 2], sem.at[blk % 2],
                    nval_ref[blk], start, span=nch)

    @pl.when(b == 0)
    def _():
        xbuf[...] = jnp.zeros(xbuf.shape, xbuf.dtype)
        gather(b, True)

    @pl.when(b + 1 < nu_ref[0])
    def _():
        gather(b + 1, True)

    _expert_weights(be_ref, nu_ref, ws_ref, nx_ref, (wg_hbm, wu_hbm), wbuf, wsem, (wgb_ref, wub_ref), layer)

    @pl.when(b < nu_ref[0])
    def _():
        gather(b, False)
        xb = xbuf.at[b % 2]
        x = jnp.concatenate([xb[pl.ds(j, MOE_BM, stride=nch), :] for j in range(nch)], axis=1).astype(BF16)
        hg = jnp.dot(x, wgb_ref[...], preferred_element_type=F32)
        hu = jnp.dot(x, wub_ref[...], preferred_element_type=F32)
        o_ref[...] = (_silu(hg) * hu).astype(o_ref.dtype)

    @pl.when(b >= nu_ref[0])
    def _():
        o_ref[...] = jnp.zeros(o_ref.shape, o_ref.dtype)


def _moe_down_kernel(be_ref, nu_ref, ws_ref, nx_ref, h_ref, wd_hbm, o_ref, wdb_ref, wbuf, wsem, *, layer):
    b = pl.program_id(0)
    _expert_weights(be_ref, nu_ref, ws_ref, nx_ref, (wd_hbm,), wbuf, wsem, (wdb_ref,), layer)

    @pl.when(b < nu_ref[0])
    def _():
        o_ref[...] = jnp.dot(h_ref[...], wdb_ref[...], preferred_element_type=F32)

    @pl.when(b >= nu_ref[0])
    def _():
        o_ref[...] = jnp.zeros(o_ref.shape, o_ref.dtype)


def _moe_experts(m, tok_sorted, blk_expert, blk_base, blk_nval, n_used, blk_wslot, blk_next, w_gate, w_up,
                 w_down, layer):
    d = w_gate.shape[2]
    n_blk = blk_expert.shape[0]
    n_slot = n_blk * MOE_BM
    de = w_gate.shape[3]
    hbm = pl.BlockSpec(memory_space=pl.ANY)
    hid = pl.pallas_call(
        functools.partial(_moe_up_kernel, layer=layer),
        grid_spec=pltpu.PrefetchScalarGridSpec(
            num_scalar_prefetch=7, grid=(n_blk,),
            in_specs=[hbm, hbm, hbm],
            out_specs=pl.BlockSpec((MOE_BM, de), lambda b, *_: (b, 0)),
            scratch_shapes=[pltpu.VMEM((d, de), BF16), pltpu.VMEM((d, de), BF16),
                            pltpu.VMEM((2, MOE_BM * (d // LANES), LANES), F32), pltpu.SemaphoreType.DMA((2,)),
                            pltpu.VMEM((2, 2, d, de), F32), pltpu.SemaphoreType.DMA((2,))]),
        out_shape=jax.ShapeDtypeStruct((n_slot, de), BF16),
        compiler_params=_cp("arbitrary"),
        name="moe_up",
    )(blk_expert, n_used, blk_wslot, blk_next, blk_base, blk_nval, tok_sorted, m, w_gate, w_up)
    return pl.pallas_call(
        functools.partial(_moe_down_kernel, layer=layer),
        grid_spec=pltpu.PrefetchScalarGridSpec(
            num_scalar_prefetch=4, grid=(n_blk,),
            in_specs=[pl.BlockSpec((MOE_BM, de), lambda b, *_: (b, 0)), hbm],
            out_specs=pl.BlockSpec((MOE_BM, d), lambda b, *_: (b, 0)),
            scratch_shapes=[pltpu.VMEM((de, d), BF16), pltpu.VMEM((2, 1, de, d), F32),
                            pltpu.SemaphoreType.DMA((2,))]),
        out_shape=jax.ShapeDtypeStruct((n_slot, d), F32),
        compiler_params=_cp("arbitrary"),
        name="moe_down",
    )(blk_expert, n_used, blk_wslot, blk_next, hid, w_down)


def _route(logits):
    n_exp = N_GROUPS * EXPERTS_PER_GROUP
    t = logits.shape[0]
    g_logits = logits[:, :N_GROUPS]
    g_prob = jax.nn.softmax(g_logits, axis=-1)
    g_sel = jnp.argmax(g_logits, axis=-1).astype(jnp.int32)
    g_w = jnp.take_along_axis(g_prob, g_sel[:, None], axis=1)[:, 0]
    e_logits = logits[:, N_GROUPS:N_GROUPS + n_exp].reshape(t, N_GROUPS, EXPERTS_PER_GROUP)
    e_logits = jnp.take_along_axis(e_logits, g_sel[:, None, None], axis=1)[:, 0]
    top_v, top_i = lax.top_k(e_logits, TOP_K)
    weights = g_w[:, None] * jax.nn.softmax(top_v, axis=-1)
    expert = g_sel[:, None] * EXPERTS_PER_GROUP + top_i.astype(jnp.int32)
    return expert, weights


def _dispatch(expert, weights):
    n_exp = N_GROUPS * EXPERTS_PER_GROUP
    t = expert.shape[0]
    n_as = t * TOP_K
    eid = expert.reshape(-1)
    aidx = jnp.arange(n_as, dtype=jnp.int32)
    eid_s, order = lax.sort((eid, aidx), num_keys=1, is_stable=True)
    onehot = eid_s[:, None] == jnp.arange(n_exp, dtype=jnp.int32)[None, :]
    counts = jnp.sum(onehot, axis=0, dtype=jnp.int32)
    padded = (counts + MOE_BM - 1) // MOE_BM * MOE_BM
    start = jnp.cumsum(counts) - counts
    pend = jnp.cumsum(padded)
    pstart = pend - padded
    dest = aidx + jnp.sum(jnp.where(onehot, (pstart - start)[None, :], 0), axis=1)
    n_blk = -(-n_as // MOE_BM) + n_exp
    blk_expert = jnp.searchsorted(pend, jnp.arange(n_blk, dtype=jnp.int32) * MOE_BM, side='right')
    blk_expert = jnp.minimum(blk_expert, n_exp - 1).astype(jnp.int32)
    n_used = (pend[-1] // MOE_BM).astype(jnp.int32).reshape(1)
    blk_first = jnp.arange(n_blk, dtype=jnp.int32) * MOE_BM - pstart[blk_expert]
    blk_base = jnp.clip(start[blk_expert] + blk_first, 0, n_as - 1).astype(jnp.int32)
    blk_nval = jnp.clip(counts[blk_expert] - blk_first, 0, MOE_BM).astype(jnp.int32)
    blk_base = jnp.minimum(blk_base, n_as - jnp.maximum(blk_nval, 1))
    _, slot_of = lax.sort((order, dest), num_keys=1)
    slot_idx = slot_of.reshape(t, TOP_K).T.reshape(-1)
    used = counts > 0
    ids = jnp.arange(n_exp, dtype=jnp.int32)
    wslot_e = (jnp.cumsum(used.astype(jnp.int32)) - 1) % 2
    later = jnp.where(used[None, :] & (ids[None, :] > ids[:, None]), ids[None, :], n_exp)
    next_e = jnp.min(later, axis=1)
    next_e = jnp.where(next_e == n_exp, -1, next_e).astype(jnp.int32)
    blk_wslot = jnp.maximum(wslot_e, 0)[blk_expert].astype(jnp.int32)
    blk_next = next_e[blk_expert]
    return order // TOP_K, blk_expert, blk_base, blk_nval, n_used, blk_wslot, blk_next, slot_idx


def _moe(m, logits, w_gate, w_up, w_down, layer):
    expert, weights = _route(logits)
    tok_sorted, blk_expert, blk_base, blk_nval, n_used, blk_wslot, blk_next, slot_idx = _dispatch(expert, weights)
    ys = _moe_experts(m, tok_sorted, blk_expert, blk_base, blk_nval, n_used, blk_wslot, blk_next,
                      w_gate, w_up, w_down, layer)
    return ys, slot_idx, weights.T[:, :, None]


def _diff_lambda(lam_p, layer_idx):
    lam_init = 0.8 - 0.6 * math.exp(-0.3 * layer_idx)
    lf = lam_p.astype(F32)
    lam = jnp.exp(jnp.sum(lf[0] * lf[1])) - jnp.exp(jnp.sum(lf[2] * lf[3])) + lam_init
    return jnp.stack([lam, jnp.asarray(1.0 - lam_init, F32)]).astype(F32)


def kernel(x, c, ctx, c_ctx, w_mod, b_mod, norm1_g, norm2_g, att_w_in, att_w_out, att_lambda, att_subln_g,
           att_rpb, rec_w_in, rec_w_out, rec_lb_logits, rec_gnorm_g, moe_w_group, moe_b_group, moe_w_router,
           moe_b_router, moe_w_gate, moe_w_up, moe_w_down, final_norm_g):
    b_, n_lat, d = x.shape
    n_ctx = ctx.shape[1]
    assert b_ == 1 and n_ctx == ROW_TILE and n_lat % ROW_TILE == 0 and d % LANES == 0
    depth = w_mod.shape[0]
    daw, naw = DA_HEADS * LANES, NA_HEADS * LANES
    n_exp = N_GROUPS * EXPERTS_PER_GROUP

    lbp = jax.nn.softmax(rec_lb_logits.astype(F32), axis=0)
    lbs = jnp.cumsum(lbp, axis=0) - lbp[0:1]

    cvec = jnp.zeros((8, d), F32).at[0].set(c[0]).at[1].set(c_ctx)
    mods = _modulation(cvec, w_mod, b_mod)
    cos, sin = _rope_tables(n_lat, n_ctx)

    h = jnp.concatenate([x[0], ctx[0]], axis=0)
    a = _ln_mod(h, norm1_g[0], mods[0], 0, 1, n_lat)
    out = None
    for l in range(depth):
        j = l // 2
        mod = mods[l]
        if l % 2 == 0:
            qkv = _attn_in_proj(a, att_w_in, j, cos, sin, daw, naw)
            lam2 = _diff_lambda(att_lambda[j], l)
            da_lat = _diff_attn(qkv, lam2, att_subln_g[j], 0, n_lat, 0, n_lat + n_ctx, DA_HEADS)
            da_ctx = _diff_attn(qkv, lam2, att_subln_g[j], n_lat, n_ctx, n_lat, n_ctx, DA_HEADS)
            bias = _na_bias_tables(att_rpb[j], n_lat // GRID_W)
            na = _na_attn(qkv, bias, n_lat, n_ctx, DA_HEADS, NA_HEADS)
            y = _matmul(jnp.concatenate([da_lat, da_ctx], axis=0), att_w_out, j, F32, a2=na)
        else:
            p = _matmul(a, rec_w_in, j, F32)
            o_f = _hgrn_dir(p, lbs[l, 0], False, n_lat, HG_HEADS)
            o = _hgrn_dir(p, lbs[l, 1], True, n_lat, HG_HEADS, o_fwd=o_f, gnorm_g=rec_gnorm_g[j])
            y = _matmul(o, rec_w_out, j, F32)
        w_route = jnp.zeros((d, LANES), F32).at[:, :N_GROUPS].set(moe_w_group[l])
        w_route = w_route.at[:, N_GROUPS:N_GROUPS + n_exp].set(moe_w_router[l])
        b_route = jnp.zeros((1, LANES), F32).at[0, :N_GROUPS].set(moe_b_group[l])
        b_route = b_route.at[0, N_GROUPS:N_GROUPS + n_exp].set(moe_b_router[l])
        h, m, logits = _res_ln_mod_route(h, y, norm2_g[l], mod, 2, 3, 4, w_route, b_route, n_lat)
        ys, slot_idx, wt = _moe(m, logits, moe_w_gate, moe_w_up, moe_w_down, l)
        if l + 1 < depth:
            h, a = _res_ln_mod(h, ys, slot_idx, wt, norm1_g[l + 1], mod, 5, mods[l + 1], 0, 1, n_lat)
        else:
            out = _res_final(h, ys, slot_idx, wt, final_norm_g, mod, 5, n_lat)
    return out[None]
```
